```python
import math
import jax
import jax.numpy as jnp
from jax import lax
import numpy as np

D_MODEL = 4096
BATCH = 4
SEQ = 2048
DEPTH = 4
DEC_BATCH = 8
DEC_SEQ = 4
PAST_LEN = 8192
PAGE_SIZE = 128

N_MIXERS = 2
N_RWKV = (DEPTH + 1) // 2
N_DIFF = DEPTH // 2
NORM_EPS = 1e-6
NEG_INF = -1e30

RW_HS = 64
RW_HEADS = D_MODEL // RW_HS
RW_DECAY_LORA = max(32, int(round(1.8 * D_MODEL ** 0.5 / 32)) * 32)
RW_A_LORA = max(32, int(round(1.8 * D_MODEL ** 0.5 / 32)) * 32)
RW_V_LORA = max(32, int(round(1.3 * D_MODEL ** 0.5 / 32)) * 32)
RW_GATE_LORA = max(32, int(round(0.6 * D_MODEL ** 0.8 / 32)) * 32)
RW_GN_EPS = 64e-5

DA_HD = 128
DA_HEADS = D_MODEL // (2 * DA_HD)
DA_VD = 2 * DA_HD
DA_SCALE = DA_HD ** -0.5
ROT_DIM = DA_HD // 4
ROPE_THETA = 500000.0
Q_BLOCK = 128

N_MEM = 256
XA_HEADS = 4
XA_HD = D_MODEL // 16
XA_WIDTH = XA_HEADS * XA_HD
XA_SCALE = XA_HD ** -0.5

D_FF = ((8 * D_MODEL // 3 + 255) // 256) * 256
CONV_W = 3

kernel_name = 'hybrid_rwkv7_diffattn_convffn_step'


def rms_norm(x, g):
    xf = x.astype(jnp.float32)
    y = xf * lax.rsqrt(jnp.mean(xf * xf, axis=-1, keepdims=True) + NORM_EPS)
    return (y * g.astype(jnp.float32)).astype(x.dtype)


def wkv7_recurrence(r, decay, k, v, a_vec, b_vec, s0):
    def step(s, inp):
        r_t, w_t, k_t, v_t, a_t, b_t = inp
        sa = jnp.einsum('bhvk,bhk->bhv', s, a_t)
        s = s * w_t[:, :, None, :] + sa[..., None] * b_t[:, :, None, :] + v_t[..., None] * k_t[:, :, None, :]
        return s, jnp.einsum('bhvk,bhk->bhv', s, r_t)
    xs = tuple(jnp.moveaxis(t, 1, 0) for t in (r, decay, k, v, a_vec, b_vec))
    s_final, ys = lax.scan(step, s0, xs)
    return jnp.moveaxis(ys, 0, 1), s_final


def rwkv_time_mix(xn, x_prev, s0, v_first, v_res, mix, w_r, w_k, w_v, w_o, w0, w1, w2,
                  a0, a1, a2, g1, g2, k_k, k_a, r_k, ln_w, ln_b):
    B, T, C = xn.shape
    f32 = jnp.float32
    xx = jnp.concatenate([x_prev[:, None].astype(xn.dtype), xn[:, :-1]], axis=1) - xn
    xr, xw, xk, xv, xa, xg = (xn + xx * mix[m] for m in range(6))
    r = xr @ w_r
    k = xk @ w_k
    v = xv @ w_v
    w_log = -jax.nn.softplus(-(w0 + jnp.tanh(xw @ w1) @ w2).astype(f32)) - 0.5
    decay = jnp.exp(-jnp.exp(w_log))
    if v_res is None:
        v_first = v
    else:
        v0, v1, v2 = v_res
        v = v + (v_first - v) * jax.nn.sigmoid(v0 + (xv @ v1) @ v2)
    a = jax.nn.sigmoid(a0 + (xa @ a1) @ a2)
    g = jax.nn.sigmoid(xg @ g1) @ g2
    heads = lambda t: t.astype(f32).reshape(B, T, RW_HEADS, RW_HS)
    kk = heads(k * k_k)
    kk = kk / jnp.maximum(jnp.sqrt(jnp.sum(kk * kk, axis=-1, keepdims=True)), 1e-12)
    k = k * (1 + (a - 1) * k_a)
    rh, kh, vh, ah = heads(r), heads(k), heads(v), heads(a)
    y, s_final = wkv7_recurrence(rh, decay.reshape(B, T, RW_HEADS, RW_HS), kh, vh, -kk, kk * ah,
                                 s0.astype(f32))
    mu = jnp.mean(y, axis=-1, keepdims=True)
    var = jnp.mean(jnp.square(y - mu), axis=-1, keepdims=True)
    y = ((y - mu) * lax.rsqrt(var + RW_GN_EPS)).reshape(B, T, C) * ln_w + ln_b
    bonus = jnp.sum(rh * kh * r_k.astype(f32), axis=-1, keepdims=True) * vh
    y = y + bonus.reshape(B, T, C)
    out = (y * g.astype(f32)).astype(xn.dtype) @ w_o
    return out, v_first, xn[:, -1], s_final.astype(s0.dtype)


def rope_partial(x, pos):
    half = ROT_DIM // 2
    inv_freq = ROPE_THETA ** (-jnp.arange(0, ROT_DIM, 2, dtype=jnp.float32) / ROT_DIM)
    ang = pos.astype(jnp.float32)[:, None] * inv_freq[None, :]
    cos = jnp.cos(ang)[None, :, None, None, :]
    sin = jnp.sin(ang)[None, :, None, None, :]
    xf = x[..., :ROT_DIM].astype(jnp.float32)
    x1, x2 = xf[..., :half], xf[..., half:]
    rot = jnp.concatenate([x1 * cos - x2 * sin, x2 * cos + x1 * sin], axis=-1)
    return jnp.concatenate([rot.astype(x.dtype), x[..., ROT_DIM:]], axis=-1)


def diff_project(xn, w_qkv, pos):
    B, T, _ = xn.shape
    q, k, v = jnp.split(xn @ w_qkv, 3, axis=-1)
    q = rope_partial(q.reshape(B, T, DA_HEADS, 2, DA_HD), pos)
    k = rope_partial(k.reshape(B, T, DA_HEADS, 2, DA_HD), pos)
    return q, k, v.reshape(B, T, DA_HEADS, DA_VD)


def diff_lambda(lam_params, lam_init):
    lp = lam_params.astype(jnp.float32)
    return jnp.exp(jnp.sum(lp[0] * lp[1])) - jnp.exp(jnp.sum(lp[2] * lp[3])) + lam_init


def diff_weights(s, lam):
    p = jax.nn.softmax(s, axis=-1)
    return p[:, 0] - lam * p[:, 1]


def diff_attend(q, k, v, q_pos, k_pos, lam):
    s = jnp.einsum('bqhcd,bkhcd->bchqk', q, k).astype(jnp.float32) * DA_SCALE
    s = jnp.where(k_pos[None, :] <= q_pos[:, None], s, NEG_INF)
    attn = diff_weights(s, lam).astype(v.dtype)
    return jnp.einsum('bhqk,bkhv->bqhv', attn, v)


def diff_attn_prompt(q, k, v, pos, lam):
    B, T = q.shape[:2]
    nb = T // Q_BLOCK
    qb = jnp.moveaxis(q.reshape(B, nb, Q_BLOCK, DA_HEADS, 2, DA_HD), 1, 0)
    pb = pos.reshape(nb, Q_BLOCK)
    ob = lax.map(lambda a: diff_attend(a[0], k, v, a[1], pos, lam), (qb, pb))
    return jnp.moveaxis(ob, 0, 1).reshape(B, T, DA_HEADS, DA_VD)


def diff_attn_cached(q, k_new, v_new, q_pos, k_pages, v_pages, lam):
    DB = q.shape[0]
    n_past = k_pages.shape[1] * k_pages.shape[2]
    k_past = k_pages.reshape(DB, n_past, DA_HEADS, 2, DA_HD)
    v_past = v_pages.reshape(DB, n_past, DA_HEADS, DA_VD)
    s_past = jnp.einsum('bqhcd,bkhcd->bchqk', q, k_past).astype(jnp.float32) * DA_SCALE
    s_new = jnp.einsum('bqhcd,bkhcd->bchqk', q, k_new).astype(jnp.float32) * DA_SCALE
    s_new = jnp.where(q_pos[None, :] <= q_pos[:, None], s_new, NEG_INF)
    attn = diff_weights(jnp.concatenate([s_past, s_new], axis=-1), lam).astype(v_new.dtype)
    return (jnp.einsum('bhqk,bkhv->bqhv', attn[..., :n_past], v_past)
            + jnp.einsum('bhqk,bkhv->bqhv', attn[..., n_past:], v_new))


def diff_output(o, subln, lam_init, w_o):
    B, T = o.shape[:2]
    y = rms_norm(o, subln) * (1.0 - lam_init)
    return y.reshape(B, T, DA_HEADS * DA_VD) @ w_o


def mem_kv(mem, g, w_kv):
    B, M, _ = mem.shape
    k, v = jnp.split(rms_norm(mem, g) @ w_kv, 2, axis=-1)
    return k.reshape(B, M, XA_HEADS, XA_HD), v.reshape(B, M, XA_HEADS, XA_HD)


def mem_attend(xn, mk, mv, w_q, w_o):
    B, T, _ = xn.shape
    q = (xn @ w_q).reshape(B, T, XA_HEADS, XA_HD)
    s = jnp.einsum('bqhd,bmhd->bhqm', q, mk).astype(jnp.float32) * XA_SCALE
    p = jax.nn.softmax(s, axis=-1).astype(mv.dtype)
    o = jnp.einsum('bhqm,bmhd->bqhd', p, mv)
    return o.reshape(B, T, XA_WIDTH) @ w_o


def conv_ffn(xn, prev_rows, w_up, conv_w, conv_b, w_down):
    T = xn.shape[1]
    h = xn @ w_up
    hp = jnp.concatenate([prev_rows.astype(h.dtype), h], axis=1)
    c = conv_b
    for m in range(CONV_W):
        c = c + hp[:, m:m + T] * conv_w[m]
    gate, val = jnp.split(c, 2, axis=-1)
    return (jax.nn.silu(gate) * val) @ w_down, hp[:, -(CONV_W - 1):]


def setup_inputs(seed: int = 0) -> dict:
    key = jax.random.key(seed)
    ks = iter(jax.random.split(key, 80))
    f32 = jnp.float32
    nrm = lambda shape, scale: jax.random.normal(next(ks), shape, f32) * scale
    gain = lambda shape: 1.0 + 0.05 * jax.random.normal(next(ks), shape, f32)
    unif = lambda shape, lo, hi: jax.random.uniform(next(ks), shape, f32, lo, hi)
    D = D_MODEL
    F2 = 2 * D_FF
    n_pages = PAST_LEN // PAGE_SIZE
    n_used = DEC_BATCH * n_pages
    n_pool = n_used + n_used // 4
    page_table = jax.random.permutation(next(ks), n_pool)[:n_used].reshape(DEC_BATCH, n_pages).astype(jnp.int32)
    return {
        'x_prompt': nrm((BATCH, SEQ, D), 1.0),
        'x_sample': nrm((DEC_BATCH, DEC_SEQ, D), 1.0),
        'cache_k': nrm((N_DIFF, n_pool, PAGE_SIZE, DA_HEADS, 2 * DA_HD), 1.0),
        'cache_v': nrm((N_DIFF, n_pool, PAGE_SIZE, DA_HEADS, DA_VD), 1.0),
        'state_wkv': nrm((N_RWKV, DEC_BATCH, RW_HEADS, RW_HS, RW_HS), 0.1),
        'state_shift': nrm((N_RWKV, DEC_BATCH, D), 1.0),
        'cache_mem_k': nrm((DEPTH, DEC_BATCH, N_MEM, XA_HEADS, XA_HD), 1.0),
        'cache_mem_v': nrm((DEPTH, DEC_BATCH, N_MEM, XA_HEADS, XA_HD), 1.0),
        'state_ffn_conv': nrm((DEPTH, DEC_BATCH, CONV_W - 1, F2), 1.0),
        'page_table': page_table,
        'mem_prompt': nrm((BATCH, N_MEM, D), 1.0),
        'norm_mix_pre': gain((DEPTH, D)),
        'norm_mix_post': gain((DEPTH, D)),
        'norm_mem_pre': gain((DEPTH, D)),
        'norm_mem_post': gain((DEPTH, D)),
        'norm_ffn_pre': gain((DEPTH, D)),
        'norm_ffn_post': gain((DEPTH, D)),
        'mem_norm': gain((DEPTH, D)),
        'rw_mix': unif((N_RWKV, 6, D), 0.0, 1.0),
        'rw_w_r': nrm((N_RWKV, D, D), D ** -0.5),
        'rw_w_k': nrm((N_RWKV, D, D), D ** -0.5),
        'rw_w_v': nrm((N_RWKV, D, D), D ** -0.5),
        'rw_w_o': nrm((N_RWKV, D, D), D ** -0.5),
        'rw_w0': unif((N_RWKV, D), -6.0, -1.0),
        'rw_w1': nrm((N_RWKV, D, RW_DECAY_LORA), D ** -0.5),
        'rw_w2': nrm((N_RWKV, RW_DECAY_LORA, D), 0.1 * RW_DECAY_LORA ** -0.5),
        'rw_a0': nrm((N_RWKV, D), 0.1),
        'rw_a1': nrm((N_RWKV, D, RW_A_LORA), D ** -0.5),
        'rw_a2': nrm((N_RWKV, RW_A_LORA, D), 0.1 * RW_A_LORA ** -0.5),
        'rw_v0': nrm((N_RWKV - 1, D), 0.1),
        'rw_v1': nrm((N_RWKV - 1, D, RW_V_LORA), D ** -0.5),
        'rw_v2': nrm((N_RWKV - 1, RW_V_LORA, D), 0.1 * RW_V_LORA ** -0.5),
        'rw_g1': nrm((N_RWKV, D, RW_GATE_LORA), D ** -0.5),
        'rw_g2': nrm((N_RWKV, RW_GATE_LORA, D), RW_GATE_LORA ** -0.5),
        'rw_k_k': 0.85 + nrm((N_RWKV, D), 0.05),
        'rw_k_a': gain((N_RWKV, D)),
        'rw_r_k': nrm((N_RWKV, RW_HEADS, RW_HS), 0.1),
        'rw_ln_w': gain((N_RWKV, D)),
        'rw_ln_b': nrm((N_RWKV, D), 0.01),
        'da_w_qkv': nrm((N_DIFF, D, 3 * D), D ** -0.5),
        'da_w_o': nrm((N_DIFF, DA_HEADS * DA_VD, D), (DA_HEADS * DA_VD) ** -0.5),
        'da_lambda': nrm((N_DIFF, 4, DA_HD), 0.1),
        'da_subln': gain((N_DIFF, DA_VD)),
        'xa_w_q': nrm((DEPTH, D, XA_WIDTH), D ** -0.5),
        'xa_w_kv': nrm((DEPTH, D, 2 * XA_WIDTH), D ** -0.5),
        'xa_w_o': nrm((DEPTH, XA_WIDTH, D), XA_WIDTH ** -0.5),
        'ffn_w_up': nrm((DEPTH, D, F2), D ** -0.5),
        'ffn_conv_w': nrm((DEPTH, CONV_W, F2), 0.5),
        'ffn_conv_b': nrm((DEPTH, F2), 0.01),
        'ffn_w_down': nrm((DEPTH, D_FF, D), D_FF ** -0.5),
    }


def reference(x_prompt, x_sample, cache_k, cache_v, state_wkv, state_shift, cache_mem_k, cache_mem_v,
              state_ffn_conv, page_table, mem_prompt,
              norm_mix_pre, norm_mix_post, norm_mem_pre, norm_mem_post, norm_ffn_pre, norm_ffn_post, mem_norm,
              rw_mix, rw_w_r, rw_w_k, rw_w_v, rw_w_o, rw_w0, rw_w1, rw_w2, rw_a0, rw_a1, rw_a2,
              rw_v0, rw_v1, rw_v2, rw_g1, rw_g2, rw_k_k, rw_k_a, rw_r_k, rw_ln_w, rw_ln_b,
              da_w_qkv, da_w_o, da_lambda, da_subln,
              xa_w_q, xa_w_kv, xa_w_o,
              ffn_w_up, ffn_conv_w, ffn_conv_b, ffn_w_down):
    B, T, _ = x_prompt.shape
    DB, DS, _ = x_sample.shape
    pos_p = jnp.arange(T)
    pos_s = PAST_LEN + jnp.arange(DS)
    hp, hs = x_prompt, x_sample
    vf_p = None
    vf_s = None
    k_p, v_p, k_s, v_s = [], [], [], []
    wkv_p, wkv_s, sh_p, sh_s = [], [], [], []
    mk_list, mv_list, cv_p, cv_s = [], [], [], []
    for i in range(DEPTH):
        j = i // N_MIXERS
        xn_p = rms_norm(hp, norm_mix_pre[i])
        xn_s = rms_norm(hs, norm_mix_pre[i])
        if i % N_MIXERS == 0:
            v_res = None if j == 0 else (rw_v0[j - 1], rw_v1[j - 1], rw_v2[j - 1])
            w = (rw_mix[j], rw_w_r[j], rw_w_k[j], rw_w_v[j], rw_w_o[j], rw_w0[j], rw_w1[j], rw_w2[j],
                 rw_a0[j], rw_a1[j], rw_a2[j], rw_g1[j], rw_g2[j], rw_k_k[j], rw_k_a[j], rw_r_k[j],
                 rw_ln_w[j], rw_ln_b[j])
            mix_p, vf_p, last_p, st_p = rwkv_time_mix(
                xn_p, jnp.zeros((B, D_MODEL), xn_p.dtype),
                jnp.zeros((B, RW_HEADS, RW_HS, RW_HS), jnp.float32), vf_p, v_res, *w)
            mix_s, vf_s, last_s, st_s = rwkv_time_mix(xn_s, state_shift[j], state_wkv[j], vf_s, v_res, *w)
            wkv_p.append(st_p)
            wkv_s.append(st_s)
            sh_p.append(last_p)
            sh_s.append(last_s)
        else:
            lam_init = 0.8 - 0.6 * math.exp(-0.3 * i)
            lam = diff_lambda(da_lambda[j], lam_init)
            q, k, v = diff_project(xn_p, da_w_qkv[j], pos_p)
            mix_p = diff_output(diff_attn_prompt(q, k, v, pos_p, lam), da_subln[j], lam_init, da_w_o[j])
            k_p.append(k.reshape(B, T, DA_HEADS, 2 * DA_HD))
            v_p.append(v)
            q, k, v = diff_project(xn_s, da_w_qkv[j], pos_s)
            o_s = diff_attn_cached(q, k, v, pos_s, cache_k[j, page_table], cache_v[j, page_table], lam)
            mix_s = diff_output(o_s, da_subln[j], lam_init, da_w_o[j])
            k_s.append(k.reshape(DB, DS, DA_HEADS, 2 * DA_HD))
            v_s.append(v)
        hp = hp + rms_norm(mix_p, norm_mix_post[i])
        hs = hs + rms_norm(mix_s, norm_mix_post[i])
        mk, mv = mem_kv(mem_prompt, mem_norm[i], xa_w_kv[i])
        mk_list.append(mk)
        mv_list.append(mv)
        hp = hp + rms_norm(mem_attend(rms_norm(hp, norm_mem_pre[i]), mk, mv, xa_w_q[i], xa_w_o[i]),
                           norm_mem_post[i])
        hs = hs + rms_norm(mem_attend(rms_norm(hs, norm_mem_pre[i]), cache_mem_k[i], cache_mem_v[i],
                                      xa_w_q[i], xa_w_o[i]), norm_mem_post[i])
        f_p, c_p = conv_ffn(rms_norm(hp, norm_ffn_pre[i]), jnp.zeros((B, CONV_W - 1, 2 * D_FF), hp.dtype),
                            ffn_w_up[i], ffn_conv_w[i], ffn_conv_b[i], ffn_w_down[i])
        f_s, c_s = conv_ffn(rms_norm(hs, norm_ffn_pre[i]), state_ffn_conv[i],
                            ffn_w_up[i], ffn_conv_w[i], ffn_conv_b[i], ffn_w_down[i])
        cv_p.append(c_p)
        cv_s.append(c_s)
        hp = hp + rms_norm(f_p, norm_ffn_post[i])
        hs = hs + rms_norm(f_s, norm_ffn_post[i])
    return (hp, hs,
            jnp.stack(k_p), jnp.stack(v_p), jnp.stack(k_s), jnp.stack(v_s),
            jnp.stack(wkv_p), jnp.stack(wkv_s), jnp.stack(sh_p), jnp.stack(sh_s),
            jnp.stack(mk_list), jnp.stack(mv_list), jnp.stack(cv_p), jnp.stack(cv_s))
```

```python
import functools
import math

import jax
import jax.numpy as jnp
from jax import lax
from jax.experimental import pallas as pl
from jax.experimental.pallas import tpu as pltpu

f32 = jnp.float32
bf16 = jnp.bfloat16

NORM_EPS = 1e-6
NEG_INF = -1e30
N_MIXERS = 2
RW_HS = 64
RW_GN_EPS = 64e-5
DA_HD = 128
ROT_DIM = DA_HD // 4
ROPE_THETA = 500000.0
XA_HEADS = 4
CONV_W = 3

LANES = 128
SUBLANES = 8
VMEM_LIMIT_BYTES = 56 * 1024 * 1024

TILES = dict(
    norm_bm=128,
    mm_bm=1024, mm_bn=512, mm_bk=5504,
    mix_bt=128,
    wkv_heads=16,
    wkv_tc=128,
    attn_tq=512,
    mem_tq=512,
    ffn_bm=1024, ffn_tn=256,
)


def _cp(*sem):
    return pltpu.CompilerParams(dimension_semantics=sem, vmem_limit_bytes=VMEM_LIMIT_BYTES)


def _rms(x, g):
    return x * lax.rsqrt(jnp.mean(x * x, axis=-1, keepdims=True) + NORM_EPS) * g


def _pad_to(x, axis, size):
    if x.shape[axis] == size:
        return x
    pad = [(0, 0)] * x.ndim
    pad[axis] = (0, size - x.shape[axis])
    return jnp.pad(x, pad)


def _round_up(n, m):
    return (n + m - 1) // m * m


def _norm_kernel(x_ref, g_ref, o_ref):
    o_ref[...] = _rms(x_ref[...], g_ref[...]).astype(o_ref.dtype)


def rms_norm_rows(x, g, out_dtype=bf16):
    m, d = x.shape
    bm = min(TILES["norm_bm"], m)
    return pl.pallas_call(
        _norm_kernel,
        grid=(m // bm,),
        in_specs=[pl.BlockSpec((bm, d), lambda i: (i, 0)), pl.BlockSpec((1, d), lambda i: (0, 0))],
        out_specs=pl.BlockSpec((bm, d), lambda i: (i, 0)),
        out_shape=jax.ShapeDtypeStruct((m, d), out_dtype),
        compiler_params=_cp("parallel"), name="rms_norm",
    )(x, g.reshape(1, d))


def _resid_norm_kernel(x_ref, y_ref, gp_ref, gn_ref, xo_ref, xn_ref):
    xnew = x_ref[...] + _rms(y_ref[...], gp_ref[...])
    xo_ref[...] = xnew
    xn_ref[...] = _rms(xnew, gn_ref[...]).astype(xn_ref.dtype)


def _resid_kernel(x_ref, y_ref, gp_ref, xo_ref):
    xo_ref[...] = x_ref[...] + _rms(y_ref[...], gp_ref[...])


def resid_norm(x, y, g_post, g_next=None):
    m, d = x.shape
    bm = min(TILES["norm_bm"], m)
    row = pl.BlockSpec((bm, d), lambda i: (i, 0))
    gain = pl.BlockSpec((1, d), lambda i: (0, 0))
    if g_next is None:
        return pl.pallas_call(
            _resid_kernel, grid=(m // bm,), in_specs=[row, row, gain], out_specs=row,
            out_shape=jax.ShapeDtypeStruct((m, d), f32), compiler_params=_cp("parallel"), name="resid",
        )(x, y, g_post.reshape(1, d)), None
    return pl.pallas_call(
        _resid_norm_kernel, grid=(m // bm,), in_specs=[row, row, gain, gain], out_specs=[row, row],
        out_shape=[jax.ShapeDtypeStruct((m, d), f32), jax.ShapeDtypeStruct((m, d), bf16)],
        compiler_params=_cp("parallel"), name="resid_norm",
    )(x, y, g_post.reshape(1, d), g_next.reshape(1, d))


def _epi_none(acc):
    return acc


def _epi_tanh(acc):
    return jnp.tanh(acc)


def _epi_sigmoid(acc):
    return jax.nn.sigmoid(acc)


def _epi_rope(acc, c_ref, s1_ref, s2_ref):
    c, s1, s2 = c_ref[...], s1_ref[...], s2_ref[...]
    half = ROT_DIM // 2
    outs = []
    for grp in range(acc.shape[1] // DA_HD):
        x = acc[:, grp * DA_HD:(grp + 1) * DA_HD]
        outs.append(x * c + pltpu.roll(x, DA_HD - half, 1) * s1 + pltpu.roll(x, half, 1) * s2)
    return jnp.concatenate(outs, axis=1) if len(outs) > 1 else outs[0]


def _mm_kernel(*refs, nk, epilogue, n_extra):
    a_ref, b_ref = refs[0], refs[1]
    extra = refs[2:2 + n_extra]
    o_ref = refs[2 + n_extra]
    part = jnp.dot(a_ref[...], b_ref[...], preferred_element_type=f32)
    if nk == 1:
        o_ref[...] = epilogue(part, *extra).astype(o_ref.dtype)
        return
    acc_ref = refs[3 + n_extra]
    k = pl.program_id(2)

    @pl.when(k == 0)
    def _():
        acc_ref[...] = part

    @pl.when(k > 0)
    def _():
        acc_ref[...] += part

    @pl.when(k == nk - 1)
    def _():
        o_ref[...] = epilogue(acc_ref[...], *extra).astype(o_ref.dtype)


def _largest_tile(n, cap, unit):
    if n <= cap:
        return n
    best = None
    for t in range(unit, cap + 1, unit):
        if n % t == 0:
            best = t
    assert best is not None, (n, cap, unit)
    return best


def matmul(a, b, out_dtype=f32, epilogue=_epi_none, row_tables=(), table_rows=None):
    m, k = a.shape
    k2, n = b.shape
    assert k == k2 and a.dtype == bf16 and b.dtype == bf16
    bm = _largest_tile(m, TILES["mm_bm"], SUBLANES)
    bn = _largest_tile(n, TILES["mm_bn"], LANES)
    bk = _largest_tile(k, TILES["mm_bk"], LANES)
    nk = k // bk
    in_specs = [pl.BlockSpec((bm, bk), lambda i, j, kk: (i, kk)),
                pl.BlockSpec((bk, bn), lambda i, j, kk: (kk, j))]
    tables = []
    for t in row_tables:
        if table_rows < bm:
            assert bm % table_rows == 0
            t = jnp.tile(t, (bm // table_rows, 1))
        assert t.shape[0] % bm == 0
        nrep = t.shape[0] // bm
        tables.append(t)
        in_specs.append(pl.BlockSpec((bm, t.shape[1]), lambda i, j, kk, nrep=nrep: (i % nrep, 0)))
    scratch = [pltpu.VMEM((bm, bn), f32)] if nk > 1 else []
    return pl.pallas_call(
        functools.partial(_mm_kernel, nk=nk, epilogue=epilogue, n_extra=len(tables)),
        grid=(m // bm, n // bn, nk),
        in_specs=in_specs,
        out_specs=pl.BlockSpec((bm, bn), lambda i, j, kk: (i, j)),
        out_shape=jax.ShapeDtypeStruct((m, n), out_dtype),
        scratch_shapes=scratch,
        compiler_params=_cp("parallel", "parallel", "arbitrary"), name="matmul_" + epilogue.__name__[5:],
    )(a, b, *tables)


def _mix_kernel(h_ref, g_ref, mix_ref, prev_ref, *refs, bt, last_tile, last_row):
    outs, last_ref, carry_ref = refs[:6], refs[6], refs[7]
    tj = pl.program_id(1)
    xn = _rms(h_ref[0], g_ref[...])

    @pl.when(tj == 0)
    def _():
        carry_ref[SUBLANES - 1:SUBLANES, :] = prev_ref[0]

    prev_row = carry_ref[SUBLANES - 1:SUBLANES, :]
    row = lax.broadcasted_iota(jnp.int32, xn.shape, 0)
    shifted = jnp.where(row == 0, prev_row, pltpu.roll(xn, 1, 0))
    xx = shifted - xn
    for m_i in range(6):
        outs[m_i][0] = (xn + xx * mix_ref[m_i:m_i + 1, :]).astype(outs[m_i].dtype)
    carry_ref[...] = xn[bt - SUBLANES:bt, :]

    @pl.when(tj == last_tile)
    def _():
        last_ref[0] = xn[last_row:last_row + 1, :]


def rwkv_mix(h, g, mix, x_prev, t_real):
    b, t, d = h.shape
    bt = min(TILES["mix_bt"], t)
    seq = pl.BlockSpec((1, bt, d), lambda bi, tj: (bi, tj, 0))
    one = pl.BlockSpec((1, 1, d), lambda bi, tj: (bi, 0, 0))
    outs = pl.pallas_call(
        functools.partial(_mix_kernel, bt=bt, last_tile=(t_real - 1) // bt, last_row=(t_real - 1) % bt),
        grid=(b, t // bt),
        in_specs=[seq, pl.BlockSpec((1, d), lambda bi, tj: (0, 0)),
                  pl.BlockSpec((6, d), lambda bi, tj: (0, 0)), one],
        out_specs=[seq] * 6 + [one],
        out_shape=[jax.ShapeDtypeStruct((b, t, d), bf16)] * 6 + [jax.ShapeDtypeStruct((b, 1, d), f32)],
        scratch_shapes=[pltpu.VMEM((SUBLANES, d), f32)],
        compiler_params=_cp("parallel", "arbitrary"), name="rwkv_mix",
    )(h, g.reshape(1, d), mix, x_prev.reshape(b, 1, d))
    return outs[:6], outs[6].reshape(b, d)


def _head_sums(x, bd):
    hi = x.astype(bf16)
    lo = (x - hi.astype(f32)).astype(bf16)
    return (jnp.dot(hi, bd, preferred_element_type=f32) + jnp.dot(lo, bd, preferred_element_type=f32))


def _wkv_kernel(*refs, has_vres, heads, tc, t_real, n_chunks):
    it = iter(refs)
    r_ref, k_ref, v_ref, wl_ref, al_ref, g_ref = (next(it) for _ in range(6))
    vl_ref, vf_ref = (next(it), next(it)) if has_vres else (None, None)
    w0_ref, a0_ref = next(it), next(it)
    v0_ref = next(it) if has_vres else None
    kk_ref, ka_ref, rk_ref, lnw_ref, lnb_ref, s0_ref = (next(it) for _ in range(6))
    yg_ref, sout_ref = next(it), next(it)
    s_scr, w_s, k_s, v_s, a_s, b_s, ycol = (next(it) for _ in range(7))
    reps = [next(it) for _ in range(6)]

    pairs = heads // 2
    ci = pl.program_id(2)

    @pl.when(ci == 0)
    def _():
        s_scr[...] = s0_ref[0]

    ycol[...] = jnp.zeros(ycol.shape, f32)

    sub2 = lax.broadcasted_iota(jnp.int32, (LANES, LANES), 0)
    lane2 = lax.broadcasted_iota(jnp.int32, (LANES, LANES), 1)
    bd = (sub2 // RW_HS == lane2 // RW_HS).astype(bf16)

    for p in range(pairs):
        sl = slice(p * LANES, (p + 1) * LANES)
        k = k_ref[0, :, sl]
        v = v_ref[0, :, sl]
        z = -(w0_ref[:, sl] + wl_ref[0, :, sl])
        softplus = jnp.maximum(z, 0.0) + jnp.log1p(jnp.exp(-jnp.abs(z)))
        w_s[:, sl] = jnp.exp(-jnp.exp(-softplus - 0.5))
        a = jax.nn.sigmoid(a0_ref[:, sl] + al_ref[0, :, sl])
        if has_vres:
            v = v + (vf_ref[0, :, sl] - v) * jax.nn.sigmoid(v0_ref[:, sl] + vl_ref[0, :, sl])
        kk = k * kk_ref[:, sl]
        kk = kk / jnp.maximum(jnp.sqrt(_head_sums(kk * kk, bd)), 1e-12)
        k_s[:, sl] = k * (1.0 + (a - 1.0) * ka_ref[:, sl])
        v_s[:, sl] = v
        a_s[:, sl] = -kk
        b_s[:, sl] = kk * a

    def replicate(q, carry):
        base = pl.multiple_of(q * SUBLANES, SUBLANES)
        for p in range(pairs):
            sl = slice(p * LANES, (p + 1) * LANES)
            tiles = [r_ref[0, pl.ds(base, SUBLANES), sl]] + [
                src[pl.ds(base, SUBLANES), sl] for src in (w_s, k_s, v_s, a_s, b_s)]
            for tile, dst in zip(tiles, reps):
                for i in range(SUBLANES):
                    off = pl.multiple_of((base + i) * SUBLANES, SUBLANES)
                    dst[pl.ds(off, SUBLANES), sl] = jnp.broadcast_to(tile[i:i + 1, :], (SUBLANES, LANES))
        return carry

    lax.fori_loop(0, tc // SUBLANES, replicate, 0)

    tile3 = (RW_HS // SUBLANES, SUBLANES, LANES)
    val = (lax.broadcasted_iota(jnp.int32, tile3, 0) * SUBLANES + lax.broadcasted_iota(jnp.int32, tile3, 1))
    lane = lax.broadcasted_iota(jnp.int32, tile3, 2)
    eye = [(lane == val).astype(f32), (lane == val + RW_HS).astype(f32)]
    lrow = lax.broadcasted_iota(jnp.int32, (SUBLANES, LANES), 1)
    live = [lrow < RW_HS, lrow >= RW_HS]
    nvb = RW_HS // SUBLANES

    def step(t, carry):
        off = pl.multiple_of(t * SUBLANES, SUBLANES)
        for p in range(pairs):
            sl = slice(p * LANES, (p + 1) * LANES)
            r8, w8, k8, v8, a8, b8 = (rep[pl.ds(off, SUBLANES), sl] for rep in reps)
            for par in range(2):
                h = 2 * p + par
                b_m = jnp.where(live[par], b8, 0.0)
                k_m = jnp.where(live[par], k8, 0.0)
                s = s_scr[h]
                sa = jnp.sum(s * a8[None], axis=-1, keepdims=True)
                v_col = jnp.sum(eye[par] * v8[None], axis=-1, keepdims=True)
                s = s * w8[None] + sa * b_m[None] + v_col * k_m[None]
                s_scr[h] = s
                y = jnp.sum(s * r8[None], axis=-1, keepdims=True)
                rows = slice(par * nvb, (par + 1) * nvb)
                ycol[p, rows] = jnp.where(lane == t, y, ycol[p, rows])
        return carry

    if t_real % tc == 0:
        lax.fori_loop(0, tc, step, 0)
    else:
        lax.fori_loop(0, jnp.clip(t_real - ci * tc, 0, tc), step, 0)

    for p in range(pairs):
        sl = slice(p * LANES, (p + 1) * LANES)
        y = ycol[p].reshape(LANES, LANES).T[:tc, :]
        mu = _head_sums(y, bd) * (1.0 / RW_HS)
        d = y - mu
        var = _head_sums(d * d, bd) * (1.0 / RW_HS)
        yn = d * lax.rsqrt(var + RW_GN_EPS) * lnw_ref[:, sl] + lnb_ref[:, sl]
        bonus = _head_sums(r_ref[0, :, sl] * k_s[:, sl] * rk_ref[:, sl], bd) * v_s[:, sl]
        yg_ref[0, :, sl] = ((yn + bonus) * g_ref[0, :, sl]).astype(yg_ref.dtype)

    @pl.when(ci == n_chunks - 1)
    def _():
        sout_ref[0] = s_scr[...]


def _pack_state(s):
    z = jnp.zeros_like(s)
    odd = (jnp.arange(s.shape[1]) % 2 == 1)[None, :, None, None]
    sp = jnp.where(odd, jnp.concatenate([z, s], -1), jnp.concatenate([s, z], -1))
    return sp.reshape(s.shape[0], s.shape[1], RW_HS // SUBLANES, SUBLANES, LANES)


def _unpack_state(sp):
    sp = sp.reshape(sp.shape[0], sp.shape[1], RW_HS, LANES)
    odd = (jnp.arange(sp.shape[1]) % 2 == 1)[None, :, None, None]
    return jnp.where(odd, sp[..., RW_HS:], sp[..., :RW_HS])


def wkv7(r, k, v, wl, al, g, vres, rows, s0, t_real):
    b, t, d = r.shape
    nh = d // RW_HS
    heads = min(TILES["wkv_heads"], nh)
    gc = heads * RW_HS
    tc = min(TILES["wkv_tc"], t)
    assert tc <= LANES and tc % SUBLANES == 0 and t % tc == 0
    n_chunks = t // tc
    assert t_real % tc == 0 or n_chunks == 1
    has_vres = vres is not None
    nvb = RW_HS // SUBLANES
    seq = pl.BlockSpec((1, tc, gc), lambda bi, hi, ci: (bi, ci, hi))
    prow = pl.BlockSpec((1, gc), lambda bi, hi, ci: (0, hi))
    st = pl.BlockSpec((1, heads, nvb, SUBLANES, LANES), lambda bi, hi, ci: (bi, hi, 0, 0, 0))
    seq_in = [r, k, v, wl, al, g] + (list(vres) if has_vres else [])
    names = ["w0", "a0"] + (["v0"] if has_vres else []) + ["k_k", "k_a", "r_k", "ln_w", "ln_b"]
    row_in = [rows[nm].reshape(1, d) for nm in names]
    yg, s_out = pl.pallas_call(
        functools.partial(_wkv_kernel, has_vres=has_vres, heads=heads, tc=tc, t_real=t_real, n_chunks=n_chunks),
        grid=(b, nh // heads, n_chunks),
        in_specs=[seq] * len(seq_in) + [prow] * len(row_in) + [st],
        out_specs=[seq, st],
        out_shape=[jax.ShapeDtypeStruct((b, t, d), bf16),
                   jax.ShapeDtypeStruct((b, nh, nvb, SUBLANES, LANES), f32)],
        scratch_shapes=[pltpu.VMEM((heads, nvb, SUBLANES, LANES), f32)] + [pltpu.VMEM((tc, gc), f32)] * 5
        + [pltpu.VMEM((heads // 2, 2 * nvb, SUBLANES, LANES), f32)]
        + [pltpu.VMEM((tc * SUBLANES, gc), f32)] * 6,
        compiler_params=_cp("parallel", "parallel", "arbitrary"), name="wkv7",
    )(*seq_in, *row_in, _pack_state(s0.astype(f32)))
    return yg, _unpack_state(s_out)


def _diff_lambda(lam_ref, lam_init):
    lp = lam_ref[...]
    s1 = jnp.sum(lp[0:1] * lp[1:2], axis=-1, keepdims=True)
    s2 = jnp.sum(lp[2:3] * lp[3:4], axis=-1, keepdims=True)
    return jnp.exp(s1) - jnp.exp(s2) + lam_init


def _online_softmax_step(s, v_bf, m_ref, l_ref, acc_ref, c):
    m_prev = m_ref[c]
    m_new = jnp.maximum(m_prev, jnp.max(s, axis=-1, keepdims=True))
    alpha = jnp.exp(m_prev - m_new)
    p = jnp.exp(s - m_new)
    l_ref[c] = alpha * l_ref[c] + jnp.sum(p, axis=-1, keepdims=True)
    acc_ref[c] = alpha * acc_ref[c] + jnp.dot(p.astype(bf16), v_bf, preferred_element_type=f32)
    m_ref[c] = m_new


def _diff_prompt_kernel(q_ref, k_ref, v_ref, lam_ref, sub_ref, o_ref, m_ref, l_ref, acc_ref,
                        *, tq, nk, lam_init, scale):
    i, j = pl.program_id(2), pl.program_id(3)

    @pl.when(j == 0)
    def _():
        m_ref[...] = jnp.full(m_ref.shape, NEG_INF, f32)
        l_ref[...] = jnp.zeros(l_ref.shape, f32)
        acc_ref[...] = jnp.zeros(acc_ref.shape, f32)

    @pl.when(j <= i)
    def _():
        v_bf = v_ref[0].astype(bf16)
        q_pos = i * tq + lax.broadcasted_iota(jnp.int32, (tq, tq), 0)
        k_pos = j * tq + lax.broadcasted_iota(jnp.int32, (tq, tq), 1)
        for c in range(2):
            qc = q_ref[0, :, c * DA_HD:(c + 1) * DA_HD]
            kc = k_ref[0, :, c * DA_HD:(c + 1) * DA_HD].astype(bf16)
            s = lax.dot_general(qc, kc, (((1,), (1,)), ((), ())), preferred_element_type=f32) * scale
            s = jnp.where(k_pos <= q_pos, s, NEG_INF)
            _online_softmax_step(s, v_bf, m_ref, l_ref, acc_ref, c)

    @pl.when(j == nk - 1)
    def _():
        lam = _diff_lambda(lam_ref, lam_init)
        o = acc_ref[0] / l_ref[0] - lam * (acc_ref[1] / l_ref[1])
        o_ref[0] = (_rms(o, sub_ref[...]) * (1.0 - lam_init)).astype(o_ref.dtype)


def diff_attn_prompt(q, k, v, lam_params, subln, lam_init):
    b, t, d = q.shape
    hw = 2 * DA_HD
    nh = d // hw
    tq = min(TILES["attn_tq"], t)
    nq = t // tq
    qspec = pl.BlockSpec((1, tq, hw), lambda bi, hi, i, j: (bi, i, hi))
    kspec = pl.BlockSpec((1, tq, hw), lambda bi, hi, i, j: (bi, jnp.minimum(j, i), hi))
    return pl.pallas_call(
        functools.partial(_diff_prompt_kernel, tq=tq, nk=nq, lam_init=lam_init, scale=DA_HD ** -0.5),
        grid=(b, nh, nq, nq),
        in_specs=[qspec, kspec, kspec,
                  pl.BlockSpec((4, DA_HD), lambda bi, hi, i, j: (0, 0)),
                  pl.BlockSpec((1, hw), lambda bi, hi, i, j: (0, 0))],
        out_specs=qspec,
        out_shape=jax.ShapeDtypeStruct((b, t, d), bf16),
        scratch_shapes=[pltpu.VMEM((2, tq, 1), f32), pltpu.VMEM((2, tq, 1), f32), pltpu.VMEM((2, tq, hw), f32)],
        compiler_params=_cp("parallel", "parallel", "parallel", "arbitrary"), name="diff_attn_prompt",
    )(q, k, v, lam_params, subln.reshape(1, hw))


def _diff_cached_kernel(pt_ref, q_ref, kp_ref, vp_ref, kn_ref, vn_ref, lam_ref, sub_ref, o_ref,
                        m_ref, l_ref, acc_ref, *, n_pages, nh, tpad, t_real, lam_init, scale):
    p = pl.program_id(1)
    hw = 2 * DA_HD
    rows = q_ref.shape[1]

    @pl.when(p == 0)
    def _():
        m_ref[...] = jnp.full(m_ref.shape, NEG_INF, f32)
        l_ref[...] = jnp.zeros(l_ref.shape, f32)
        acc_ref[...] = jnp.zeros(acc_ref.shape, f32)

    def update(k_f32, v_f32, mask):
        s = lax.dot_general(q_ref[0], k_f32.astype(bf16), (((1,), (1,)), ((), ())),
                            preferred_element_type=f32) * scale
        if mask is not None:
            s = jnp.where(mask, s, NEG_INF)
        _online_softmax_step(s, v_f32.astype(bf16), m_ref, l_ref, acc_ref, 0)

    @pl.when(p < n_pages)
    def _():
        update(kp_ref[...], vp_ref[...], None)

    @pl.when(p == n_pages)
    def _():
        nkeys = kn_ref.shape[1]
        qi = lax.broadcasted_iota(jnp.int32, (rows, nkeys), 0) % tpad
        kj = lax.broadcasted_iota(jnp.int32, (rows, nkeys), 1)
        update(kn_ref[0], vn_ref[0], (kj <= qi) & (kj < t_real))
        lam = _diff_lambda(lam_ref, lam_init)
        for h in range(nh):
            r0 = h * 2 * tpad
            cols = slice(h * hw, (h + 1) * hw)
            o1 = acc_ref[0, r0:r0 + tpad, cols] / l_ref[0, r0:r0 + tpad, :]
            o2 = acc_ref[0, r0 + tpad:r0 + 2 * tpad, cols] / l_ref[0, r0 + tpad:r0 + 2 * tpad, :]
            o_ref[0, :, cols] = (_rms(o1 - lam * o2, sub_ref[...]) * (1.0 - lam_init)).astype(o_ref.dtype)


def diff_attn_cached(q, k_new, v_new, cache_k, cache_v, layer, page_table, lam_params, subln, lam_init, t_real):
    b, tpad, d = q.shape
    hw = 2 * DA_HD
    nh = d // hw
    n_pages = page_table.shape[1]
    page = cache_k.shape[2]
    rows = nh * 2 * tpad
    q5 = q.reshape(b, tpad, nh * 2, DA_HD)
    eye = jnp.eye(nh * 2, dtype=q.dtype)
    q_blk = (jnp.swapaxes(q5, 1, 2)[:, :, :, None, :] * eye[None, :, None, :, None]).reshape(b, rows, d)
    kn = _pad_to(k_new, 1, page)
    vn = _pad_to(v_new, 1, page)
    ck = cache_k.reshape(cache_k.shape[0], cache_k.shape[1], page, d)
    cv = cache_v.reshape(cache_v.shape[0], cache_v.shape[1], page, d)
    last = n_pages - 1
    pspec = pl.BlockSpec((None, None, page, d), lambda bi, p, pt: (layer, pt[bi, jnp.minimum(p, last)], 0, 0))
    nspec = pl.BlockSpec((1, page, d), lambda bi, p, pt: (bi, 0, 0))
    grid_spec = pltpu.PrefetchScalarGridSpec(
        num_scalar_prefetch=1,
        grid=(b, n_pages + 1),
        in_specs=[pl.BlockSpec((1, rows, d), lambda bi, p, pt: (bi, 0, 0)), pspec, pspec, nspec, nspec,
                  pl.BlockSpec((4, DA_HD), lambda bi, p, pt: (0, 0)),
                  pl.BlockSpec((1, hw), lambda bi, p, pt: (0, 0))],
        out_specs=pl.BlockSpec((1, tpad, d), lambda bi, p, pt: (bi, 0, 0)),
        scratch_shapes=[pltpu.VMEM((1, rows, 1), f32), pltpu.VMEM((1, rows, 1), f32),
                        pltpu.VMEM((1, rows, d), f32)],
    )
    return pl.pallas_call(
        functools.partial(_diff_cached_kernel, n_pages=n_pages, nh=nh, tpad=tpad, t_real=t_real,
                          lam_init=lam_init, scale=DA_HD ** -0.5),
        grid_spec=grid_spec,
        out_shape=jax.ShapeDtypeStruct((b, tpad, d), bf16),
        compiler_params=_cp("parallel", "arbitrary"), name="diff_attn_cached",
    )(page_table, q_blk, ck, cv, kn, vn, lam_params, subln.reshape(1, hw))


def _mem_attn_kernel(q_ref, k_ref, v_ref, o_ref, *, hd, scale):
    for h in range(XA_HEADS):
        cols = slice(h * hd, (h + 1) * hd)
        s = lax.dot_general(q_ref[0, :, cols], k_ref[0, :, cols].astype(bf16), (((1,), (1,)), ((), ())),
                            preferred_element_type=f32) * scale
        e = jnp.exp(s - jnp.max(s, axis=-1, keepdims=True))
        p = e / jnp.sum(e, axis=-1, keepdims=True)
        o_ref[0, :, cols] = jnp.dot(p.astype(bf16), v_ref[0, :, cols].astype(bf16),
                                    preferred_element_type=f32).astype(o_ref.dtype)


def mem_attend(q, mk, mv):
    b, t, w = q.shape
    nm = mk.shape[1]
    tq = min(TILES["mem_tq"], t)
    hd = w // XA_HEADS
    qspec = pl.BlockSpec((1, tq, w), lambda bi, i: (bi, i, 0))
    mspec = pl.BlockSpec((1, nm, w), lambda bi, i: (bi, 0, 0))
    return pl.pallas_call(
        functools.partial(_mem_attn_kernel, hd=hd, scale=hd ** -0.5),
        grid=(b, t // tq),
        in_specs=[qspec, mspec, mspec],
        out_specs=qspec,
        out_shape=jax.ShapeDtypeStruct((b, t, w), bf16),
        compiler_params=_cp("parallel", "parallel"), name="mem_attend",
    )(q, mk, mv)


def _ffn_up_kernel(a_ref, wg_ref, wv_ref, cwg_ref, cwv_ref, cbg_ref, cbv_ref, pg_ref, pv_ref,
                   act_ref, csg_ref, csv_ref, carry_ref, prev_ref,
                   *, seq_len, n_sub, tiles_per_seq, last_tile, last_row):
    i, j = pl.program_id(0), pl.program_id(1)
    a = a_ref[...]
    row = lax.broadcasted_iota(jnp.int32, (seq_len, act_ref.shape[1]), 0)

    def half(slot, w_ref, cw_ref, cb_ref, p_ref, cs_ref):
        h_all = jnp.dot(a, w_ref[...], preferred_element_type=f32)
        cw = cw_ref[...]
        outs = []
        for s in range(n_sub):
            h = h_all[s * seq_len:(s + 1) * seq_len]
            if tiles_per_seq == 1:
                p0, p1 = p_ref[s, 0:1, :], p_ref[s, 1:2, :]
            else:
                first = (i % tiles_per_seq) == 0

                @pl.when(first)
                def _():
                    prev_ref[slot] = p_ref[s]

                @pl.when(jnp.logical_not(first))
                def _():
                    prev_ref[slot] = carry_ref[j, slot, SUBLANES - 2:SUBLANES, :]

                p0, p1 = prev_ref[slot, 0:1, :], prev_ref[slot, 1:2, :]
                carry_ref[j, slot] = h[seq_len - SUBLANES:seq_len]
            r1 = jnp.where(row == 0, p1, pltpu.roll(h, 1, 0))
            r2 = jnp.where(row == 0, p0, jnp.where(row == 1, p1, pltpu.roll(h, 2, 0)))
            outs.append(cb_ref[...] + r2 * cw[0:1] + r1 * cw[1:2] + h * cw[2:3])
            cs_ref[s] = h[last_row - 1:last_row + 1]
        return jnp.concatenate(outs, axis=0) if n_sub > 1 else outs[0]

    gate = half(0, wg_ref, cwg_ref, cbg_ref, pg_ref, csg_ref)
    val = half(1, wv_ref, cwv_ref, cbv_ref, pv_ref, csv_ref)
    act_ref[...] = (gate * jax.nn.sigmoid(gate) * val).astype(act_ref.dtype)


def ffn_up(xn, w_up, conv_w, conv_b, prev_rows, t_seq, t_real):
    m, d = xn.shape
    f2 = w_up.shape[1]
    dff = f2 // 2
    nseq = m // t_seq
    bm = _largest_tile(m, TILES["ffn_bm"], SUBLANES)
    tn = _largest_tile(dff, TILES["ffn_tn"], LANES)
    nn = dff // tn
    if bm >= t_seq:
        assert bm % t_seq == 0
        n_sub, seq_len, tiles_per_seq = bm // t_seq, t_seq, 1
    else:
        assert t_seq % bm == 0
        n_sub, seq_len, tiles_per_seq = 1, bm, t_seq // bm
    last_tile, last_row = (t_real - 1) // seq_len, (t_real - 1) % seq_len
    assert last_tile == tiles_per_seq - 1 and last_row >= CONV_W - 2
    seq_of = lambda i: i // tiles_per_seq
    pspec = lambda off: pl.BlockSpec((n_sub, CONV_W - 1, tn), lambda i, j: (seq_of(i), 0, j + off))
    tail_spec = pl.BlockSpec((n_sub, CONV_W - 1, tn), lambda i, j: (i, 0, j))
    tail_shape = jax.ShapeDtypeStruct((nseq * tiles_per_seq, CONV_W - 1, dff), f32)
    cw3 = conv_w.reshape(CONV_W, f2)
    cb2 = conv_b.reshape(1, f2)
    act, csg, csv = pl.pallas_call(
        functools.partial(_ffn_up_kernel, seq_len=seq_len, n_sub=n_sub, tiles_per_seq=tiles_per_seq,
                          last_tile=last_tile, last_row=last_row),
        grid=(m // bm, nn),
        in_specs=[pl.BlockSpec((bm, d), lambda i, j: (i, 0)),
                  pl.BlockSpec((d, tn), lambda i, j: (0, j)),
                  pl.BlockSpec((d, tn), lambda i, j: (0, j + nn)),
                  pl.BlockSpec((CONV_W, tn), lambda i, j: (0, j)),
                  pl.BlockSpec((CONV_W, tn), lambda i, j: (0, j + nn)),
                  pl.BlockSpec((1, tn), lambda i, j: (0, j)),
                  pl.BlockSpec((1, tn), lambda i, j: (0, j + nn)),
                  pspec(0), pspec(nn)],
        out_specs=[pl.BlockSpec((bm, tn), lambda i, j: (i, j)), tail_spec, tail_spec],
        out_shape=[jax.ShapeDtypeStruct((m, dff), bf16), tail_shape, tail_shape],
        scratch_shapes=[pltpu.VMEM((nn, 2, SUBLANES, tn), f32), pltpu.VMEM((2, CONV_W - 1, tn), f32)],
        compiler_params=_cp("arbitrary", "arbitrary"), name="ffn_up",
    )(xn, w_up, w_up, cw3, cw3, cb2, cb2, prev_rows, prev_rows)
    tails = jnp.concatenate([csg, csv], axis=-1).reshape(nseq, tiles_per_seq, CONV_W - 1, f2)
    return act, tails[:, last_tile]


def _rope_tables(pos):
    half = ROT_DIM // 2
    inv_freq = ROPE_THETA ** (-jnp.arange(0, ROT_DIM, 2, dtype=f32) / ROT_DIM)
    ang = pos.astype(f32)[:, None] * inv_freq[None, :]
    cos, sin = jnp.cos(ang), jnp.sin(ang)
    n = pos.shape[0]
    z_half = jnp.zeros((n, half), f32)
    z_rest = jnp.zeros((n, DA_HD - ROT_DIM), f32)
    c = jnp.concatenate([cos, cos, jnp.ones((n, DA_HD - ROT_DIM), f32)], axis=1)
    s1 = jnp.concatenate([-sin, z_half, z_rest], axis=1)
    s2 = jnp.concatenate([z_half, sin, z_rest], axis=1)
    return c, s1, s2


def _group(x, t_real, states, prm, w, page_info):
    b, t, d = x.shape
    m = b * t
    depth = prm["norm_mix_pre"].shape[0]
    h = x.reshape(m, d)
    out = dict(k=[], v=[], wkv=[], shift=[], conv=[])
    v_first = None
    xn = None
    rope = _rope_tables(states["pos"])
    for i in range(depth):
        j = i // N_MIXERS
        if i % N_MIXERS == 0:
            mixes, last = rwkv_mix(h.reshape(b, t, d), prm["norm_mix_pre"][i], prm["rw_mix"][j],
                                   states["shift"][j], t_real)
            xr, xw, xk, xv, xa, xg = (z.reshape(m, d) for z in mixes)
            r = matmul(xr, w["rw_w_r"][j])
            k = matmul(xk, w["rw_w_k"][j])
            v = matmul(xv, w["rw_w_v"][j])
            wl = matmul(matmul(xw, w["rw_w1"][j], bf16, _epi_tanh), w["rw_w2"][j])
            al = matmul(matmul(xa, w["rw_a1"][j], bf16), w["rw_a2"][j])
            g = matmul(matmul(xg, w["rw_g1"][j], bf16, _epi_sigmoid), w["rw_g2"][j])
            rows = {nm: prm["rw_" + nm][j] for nm in ("w0", "a0", "k_k", "k_a", "ln_w", "ln_b")}
            rows["r_k"] = prm["rw_r_k"][j].reshape(d)
            if j == 0:
                vres = None
                v_first = v
            else:
                vl = matmul(matmul(xv, w["rw_v1"][j - 1], bf16), w["rw_v2"][j - 1])
                vres = (vl.reshape(b, t, d), v_first.reshape(b, t, d))
                rows["v0"] = prm["rw_v0"][j - 1]
            to3 = lambda z: z.reshape(b, t, d)
            yg, s_fin = wkv7(to3(r), to3(k), to3(v), to3(wl), to3(al), to3(g), vres, rows,
                             states["wkv"][j], t_real)
            mix = matmul(yg.reshape(m, d), w["rw_w_o"][j])
            out["wkv"].append(s_fin.astype(states["wkv"][j].dtype))
            out["shift"].append(last)
        else:
            lam_init = 0.8 - 0.6 * math.exp(-0.3 * i)
            wqkv = w["da_w_qkv"][j]
            q = matmul(xn, wqkv[0], bf16, _epi_rope, rope, t)
            k = matmul(xn, wqkv[1], f32, _epi_rope, rope, t)
            v = matmul(xn, wqkv[2])
            q3, k3, v3 = q.reshape(b, t, d), k.reshape(b, t, d), v.reshape(b, t, d)
            if page_info is None:
                o = diff_attn_prompt(q3, k3, v3, prm["da_lambda"][j], prm["da_subln"][j], lam_init)
            else:
                cache_k, cache_v, page_table = page_info
                o = diff_attn_cached(q3, k3, v3, cache_k, cache_v, j, page_table, prm["da_lambda"][j],
                                     prm["da_subln"][j], lam_init, t_real)
            mix = matmul(o.reshape(m, d), w["da_w_o"][j])
            out["k"].append(k3)
            out["v"].append(v3)
        h, xn = resid_norm(h, mix, prm["norm_mix_post"][i], prm["norm_mem_pre"][i])
        mk, mv = states["mem_k"][i], states["mem_v"][i]
        xw_ = mk.shape[-1] * mk.shape[-2]
        q = matmul(xn, w["xa_w_q"][i], bf16)
        o = mem_attend(q.reshape(b, t, xw_), mk.reshape(b, -1, xw_), mv.reshape(b, -1, xw_))
        xa_out = matmul(o.reshape(m, xw_), w["xa_w_o"][i])
        h, xn = resid_norm(h, xa_out, prm["norm_mem_post"][i], prm["norm_ffn_pre"][i])
        act, conv_state = ffn_up(xn, w["ffn_w_up"][i], prm["ffn_conv_w"][i], prm["ffn_conv_b"][i],
                                 states["conv"][i], t, t_real)
        f = matmul(act, w["ffn_w_down"][i])
        out["conv"].append(conv_state)
        nxt = i + 1
        g_next = prm["norm_mix_pre"][nxt] if (nxt < depth and nxt % N_MIXERS != 0) else None
        h, xn = resid_norm(h, f, prm["norm_ffn_post"][i], g_next)
    return h.reshape(b, t, d), out


def kernel(x_prompt, x_sample, cache_k, cache_v, state_wkv, state_shift, cache_mem_k, cache_mem_v, state_ffn_conv, page_table, mem_prompt, norm_mix_pre, norm_mix_post, norm_mem_pre, norm_mem_post, norm_ffn_pre, norm_ffn_post, mem_norm, rw_mix, rw_w_r, rw_w_k, rw_w_v, rw_w_o, rw_w0, rw_w1, rw_w2, rw_a0, rw_a1, rw_a2, rw_v0, rw_v1, rw_v2, rw_g1, rw_g2, rw_k_k, rw_k_a, rw_r_k, rw_ln_w, rw_ln_b, da_w_qkv, da_w_o, da_lambda, da_subln, xa_w_q, xa_w_kv, xa_w_o, ffn_w_up, ffn_conv_w, ffn_conv_b, ffn_w_down):
    b, t, d = x_prompt.shape
    db, ds, _ = x_sample.shape
    depth = norm_mix_pre.shape[0]
    n_rwkv = state_wkv.shape[0]
    nh_rw = d // RW_HS
    f2 = ffn_w_up.shape[-1]
    past_len = page_table.shape[1] * cache_k.shape[2]

    prm = dict(norm_mix_pre=norm_mix_pre, norm_mix_post=norm_mix_post, norm_mem_pre=norm_mem_pre,
               norm_mem_post=norm_mem_post, norm_ffn_pre=norm_ffn_pre, norm_ffn_post=norm_ffn_post,
               rw_mix=rw_mix, rw_w0=rw_w0, rw_a0=rw_a0, rw_v0=rw_v0, rw_k_k=rw_k_k, rw_k_a=rw_k_a,
               rw_r_k=rw_r_k, rw_ln_w=rw_ln_w, rw_ln_b=rw_ln_b, da_lambda=da_lambda, da_subln=da_subln,
               ffn_conv_w=ffn_conv_w, ffn_conv_b=ffn_conv_b)

    def lora(w_in, w_out):
        rank = _round_up(w_in.shape[-1], LANES)
        return _pad_to(w_in, 2, rank).astype(bf16), _pad_to(w_out, 1, rank).astype(bf16)

    w = dict(rw_w_r=rw_w_r.astype(bf16), rw_w_k=rw_w_k.astype(bf16), rw_w_v=rw_w_v.astype(bf16),
             rw_w_o=rw_w_o.astype(bf16), da_w_o=da_w_o.astype(bf16), xa_w_q=xa_w_q.astype(bf16),
             xa_w_kv=xa_w_kv.astype(bf16), xa_w_o=xa_w_o.astype(bf16), ffn_w_up=ffn_w_up.astype(bf16),
             ffn_w_down=ffn_w_down.astype(bf16))
    w["rw_w1"], w["rw_w2"] = lora(rw_w1, rw_w2)
    w["rw_a1"], w["rw_a2"] = lora(rw_a1, rw_a2)
    w["rw_v1"], w["rw_v2"] = lora(rw_v1, rw_v2)
    w["rw_g1"], w["rw_g2"] = lora(rw_g1, rw_g2)
    w["da_w_qkv"] = [[da_w_qkv[jj, :, c * d:(c + 1) * d].astype(bf16) for c in range(3)]
                     for jj in range(da_w_qkv.shape[0])]

    mem_rows = mem_prompt.reshape(-1, d)
    mem_k, mem_v = [], []
    for i in range(depth):
        kv = matmul(rms_norm_rows(mem_rows, mem_norm[i]), w["xa_w_kv"][i])
        xw_ = kv.shape[1] // 2
        hd = xw_ // XA_HEADS
        mem_k.append(kv[:, :xw_].reshape(b, -1, XA_HEADS, hd))
        mem_v.append(kv[:, xw_:].reshape(b, -1, XA_HEADS, hd))

    states_p = dict(pos=jnp.arange(t), shift=jnp.zeros((n_rwkv, b, d), f32),
                    wkv=jnp.zeros((n_rwkv, b, nh_rw, RW_HS, RW_HS), f32), mem_k=mem_k, mem_v=mem_v,
                    conv=jnp.zeros((depth, b, CONV_W - 1, f2), f32))
    hp, out_p = _group(x_prompt, t, states_p, prm, w, None)

    tpad = _round_up(ds, SUBLANES)
    states_s = dict(pos=past_len + jnp.arange(tpad), shift=state_shift, wkv=state_wkv,
                    mem_k=cache_mem_k, mem_v=cache_mem_v, conv=state_ffn_conv)
    hs, out_s = _group(_pad_to(x_sample, 1, tpad), ds, states_s, prm, w, (cache_k, cache_v, page_table))

    nh_da = d // (2 * DA_HD)
    heads = lambda z, bb, tt: z.reshape(bb, tt, nh_da, 2 * DA_HD)
    return (hp, hs[:, :ds],
            jnp.stack([heads(z, b, t) for z in out_p["k"]]), jnp.stack([heads(z, b, t) for z in out_p["v"]]),
            jnp.stack([heads(z, db, tpad)[:, :ds] for z in out_s["k"]]),
            jnp.stack([heads(z, db, tpad)[:, :ds] for z in out_s["v"]]),
            jnp.stack(out_p["wkv"]), jnp.stack(out_s["wkv"]),
            jnp.stack(out_p["shift"]), jnp.stack(out_s["shift"]),
            jnp.stack(mem_k), jnp.stack(mem_v),
            jnp.stack(out_p["conv"]), jnp.stack(out_s["conv"]))
```

```python
import functools
import math

import jax
import jax.numpy as jnp
from jax import lax
from jax.experimental import pallas as pl
from jax.experimental.pallas import tpu as pltpu

f32 = jnp.float32
bf16 = jnp.bfloat16

NORM_EPS = 1e-6
NEG_INF = -1e30
N_MIXERS = 2
RW_HS = 64
RW_GN_EPS = 64e-5
DA_HD = 128
ROT_DIM = DA_HD // 4
ROPE_THETA = 500000.0
XA_HEADS = 4
CONV_W = 3

LANES = 128
SUBLANES = 8
VMEM_LIMIT_BYTES = 56 * 1024 * 1024

TILES = dict(
    norm_bm=128,
    mm_bm=1024, mm_bn=512, mm_bk=5504,
    mix_bt=128,
    wkv_tc=16,
    attn_tq=512,
    mem_tq=512,
    ffn_bm=1024, ffn_tn=256,
)


def _cp(*sem):
    return pltpu.CompilerParams(dimension_semantics=sem, vmem_limit_bytes=VMEM_LIMIT_BYTES)


def _rms(x, g):
    return x * lax.rsqrt(jnp.mean(x * x, axis=-1, keepdims=True) + NORM_EPS) * g


def _pad_to(x, axis, size):
    if x.shape[axis] == size:
        return x
    pad = [(0, 0)] * x.ndim
    pad[axis] = (0, size - x.shape[axis])
    return jnp.pad(x, pad)


def _round_up(n, m):
    return (n + m - 1) // m * m


def _norm_kernel(x_ref, g_ref, o_ref):
    o_ref[...] = _rms(x_ref[...], g_ref[...]).astype(o_ref.dtype)


def rms_norm_rows(x, g, out_dtype=bf16):
    m, d = x.shape
    bm = min(TILES["norm_bm"], m)
    return pl.pallas_call(
        _norm_kernel,
        grid=(m // bm,),
        in_specs=[pl.BlockSpec((bm, d), lambda i: (i, 0)), pl.BlockSpec((1, d), lambda i: (0, 0))],
        out_specs=pl.BlockSpec((bm, d), lambda i: (i, 0)),
        out_shape=jax.ShapeDtypeStruct((m, d), out_dtype),
        compiler_params=_cp("parallel"), name="rms_norm",
    )(x, g.reshape(1, d))


def _resid_norm_kernel(x_ref, y_ref, gp_ref, gn_ref, xo_ref, xn_ref):
    xnew = x_ref[...] + _rms(y_ref[...], gp_ref[...])
    xo_ref[...] = xnew
    xn_ref[...] = _rms(xnew, gn_ref[...]).astype(xn_ref.dtype)


def _resid_kernel(x_ref, y_ref, gp_ref, xo_ref):
    xo_ref[...] = x_ref[...] + _rms(y_ref[...], gp_ref[...])


def resid_norm(x, y, g_post, g_next=None):
    m, d = x.shape
    bm = min(TILES["norm_bm"], m)
    row = pl.BlockSpec((bm, d), lambda i: (i, 0))
    gain = pl.BlockSpec((1, d), lambda i: (0, 0))
    if g_next is None:
        return pl.pallas_call(
            _resid_kernel, grid=(m // bm,), in_specs=[row, row, gain], out_specs=row,
            out_shape=jax.ShapeDtypeStruct((m, d), f32), compiler_params=_cp("parallel"), name="resid",
        )(x, y, g_post.reshape(1, d)), None
    return pl.pallas_call(
        _resid_norm_kernel, grid=(m // bm,), in_specs=[row, row, gain, gain], out_specs=[row, row],
        out_shape=[jax.ShapeDtypeStruct((m, d), f32), jax.ShapeDtypeStruct((m, d), bf16)],
        compiler_params=_cp("parallel"), name="resid_norm",
    )(x, y, g_post.reshape(1, d), g_next.reshape(1, d))


def _epi_none(acc):
    return acc


def _epi_tanh(acc):
    return jnp.tanh(acc)


def _epi_sigmoid(acc):
    return jax.nn.sigmoid(acc)


def _epi_rope(acc, c_ref, s1_ref, s2_ref):
    c, s1, s2 = c_ref[...], s1_ref[...], s2_ref[...]
    half = ROT_DIM // 2
    outs = []
    for grp in range(acc.shape[1] // DA_HD):
        x = acc[:, grp * DA_HD:(grp + 1) * DA_HD]
        outs.append(x * c + pltpu.roll(x, DA_HD - half, 1) * s1 + pltpu.roll(x, half, 1) * s2)
    return jnp.concatenate(outs, axis=1) if len(outs) > 1 else outs[0]


def _mm_kernel(*refs, nk, epilogue, n_extra):
    a_ref, b_ref = refs[0], refs[1]
    extra = refs[2:2 + n_extra]
    o_ref = refs[2 + n_extra]
    part = jnp.dot(a_ref[...], b_ref[...], preferred_element_type=f32)
    if nk == 1:
        o_ref[...] = epilogue(part, *extra).astype(o_ref.dtype)
        return
    acc_ref = refs[3 + n_extra]
    k = pl.program_id(2)

    @pl.when(k == 0)
    def _():
        acc_ref[...] = part

    @pl.when(k > 0)
    def _():
        acc_ref[...] += part

    @pl.when(k == nk - 1)
    def _():
        o_ref[...] = epilogue(acc_ref[...], *extra).astype(o_ref.dtype)


def _largest_tile(n, cap, unit):
    if n <= cap:
        return n
    best = None
    for t in range(unit, cap + 1, unit):
        if n % t == 0:
            best = t
    assert best is not None, (n, cap, unit)
    return best


def matmul(a, b, out_dtype=f32, epilogue=_epi_none, row_tables=(), table_rows=None):
    m, k = a.shape
    k2, n = b.shape
    assert k == k2 and a.dtype == bf16 and b.dtype == bf16
    bm = _largest_tile(m, TILES["mm_bm"], SUBLANES)
    bn = _largest_tile(n, TILES["mm_bn"], LANES)
    bk = _largest_tile(k, TILES["mm_bk"], LANES)
    nk = k // bk
    in_specs = [pl.BlockSpec((bm, bk), lambda i, j, kk: (i, kk)),
                pl.BlockSpec((bk, bn), lambda i, j, kk: (kk, j))]
    tables = []
    for t in row_tables:
        if table_rows < bm:
            assert bm % table_rows == 0
            t = jnp.tile(t, (bm // table_rows, 1))
        assert t.shape[0] % bm == 0
        nrep = t.shape[0] // bm
        tables.append(t)
        in_specs.append(pl.BlockSpec((bm, t.shape[1]), lambda i, j, kk, nrep=nrep: (i % nrep, 0)))
    scratch = [pltpu.VMEM((bm, bn), f32)] if nk > 1 else []
    return pl.pallas_call(
        functools.partial(_mm_kernel, nk=nk, epilogue=epilogue, n_extra=len(tables)),
        grid=(m // bm, n // bn, nk),
        in_specs=in_specs,
        out_specs=pl.BlockSpec((bm, bn), lambda i, j, kk: (i, j)),
        out_shape=jax.ShapeDtypeStruct((m, n), out_dtype),
        scratch_shapes=scratch,
        compiler_params=_cp("parallel", "parallel", "arbitrary"), name="matmul_" + epilogue.__name__[5:],
    )(a, b, *tables)


def _mix_kernel(h_ref, g_ref, mix_ref, prev_ref, *refs, bt, last_tile, last_row):
    outs, last_ref, carry_ref = refs[:6], refs[6], refs[7]
    tj = pl.program_id(1)
    xn = _rms(h_ref[0], g_ref[...])

    @pl.when(tj == 0)
    def _():
        carry_ref[SUBLANES - 1:SUBLANES, :] = prev_ref[0]

    prev_row = carry_ref[SUBLANES - 1:SUBLANES, :]
    row = lax.broadcasted_iota(jnp.int32, xn.shape, 0)
    shifted = jnp.where(row == 0, prev_row, pltpu.roll(xn, 1, 0))
    xx = shifted - xn
    for m_i in range(6):
        outs[m_i][0] = (xn + xx * mix_ref[m_i:m_i + 1, :]).astype(outs[m_i].dtype)
    carry_ref[...] = xn[bt - SUBLANES:bt, :]

    @pl.when(tj == last_tile)
    def _():
        last_ref[0] = xn[last_row:last_row + 1, :]


def rwkv_mix(h, g, mix, x_prev, t_real):
    b, t, d = h.shape
    bt = min(TILES["mix_bt"], t)
    seq = pl.BlockSpec((1, bt, d), lambda bi, tj: (bi, tj, 0))
    one = pl.BlockSpec((1, 1, d), lambda bi, tj: (bi, 0, 0))
    outs = pl.pallas_call(
        functools.partial(_mix_kernel, bt=bt, last_tile=(t_real - 1) // bt, last_row=(t_real - 1) % bt),
        grid=(b, t // bt),
        in_specs=[seq, pl.BlockSpec((1, d), lambda bi, tj: (0, 0)),
                  pl.BlockSpec((6, d), lambda bi, tj: (0, 0)), one],
        out_specs=[seq] * 6 + [one],
        out_shape=[jax.ShapeDtypeStruct((b, t, d), bf16)] * 6 + [jax.ShapeDtypeStruct((b, 1, d), f32)],
        scratch_shapes=[pltpu.VMEM((SUBLANES, d), f32)],
        compiler_params=_cp("parallel", "arbitrary"), name="rwkv_mix",
    )(h, g.reshape(1, d), mix, x_prev.reshape(b, 1, d))
    return outs[:6], outs[6].reshape(b, d)


def _wkv_kernel(*refs, has_vres, tc, t_real, n_chunks):
    it = iter(refs)
    r_ref, k_ref, v_ref, wl_ref, al_ref, g_ref = (next(it) for _ in range(6))
    vl_ref, vf_ref = (next(it), next(it)) if has_vres else (None, None)
    w0_ref, a0_ref = next(it), next(it)
    v0_ref = next(it) if has_vres else None
    kk_ref, ka_ref, rk_ref, lnw_ref, lnb_ref, s0_ref = (next(it) for _ in range(6))
    yg_ref, sout_ref = next(it), next(it)
    s_scr = next(it)
    r_t, w_t, k_t, v_t, a_t, b_t, y_t = (next(it) for _ in range(7))
    v_ct = next(it)

    nvb = RW_HS // SUBLANES
    n_cp = RW_HS // 2
    ci = pl.program_id(1)

    @pl.when(ci == 0)
    def _():
        s_scr[...] = s0_ref[0]

    y_t[...] = jnp.zeros(y_t.shape, f32)

    low = lax.broadcasted_iota(jnp.int32, (tc, LANES), 1) < RW_HS

    def gather(ref, cp):
        cols = slice(cp * LANES, (cp + 1) * LANES)
        g0, g1 = ref[0, :, cols], ref[1, :, cols]
        return (jnp.where(low, g0, pltpu.roll(g1, RW_HS, 1)), jnp.where(low, pltpu.roll(g0, RW_HS, 1), g1))

    def scatter(x_even, x_odd):
        return (jnp.where(low, x_even, pltpu.roll(x_odd, RW_HS, 1)), jnp.where(low, pltpu.roll(x_even, RW_HS, 1), x_odd))

    def prow(ref, c):
        return ref[c:c + 1, :]

    ss = jnp.zeros((tc, LANES), f32)
    for cp in range(n_cp):
        for par, kc in enumerate(gather(k_ref, cp)):
            kk = kc * prow(kk_ref, 2 * cp + par)
            ss = ss + kk * kk
    kk_den = jnp.maximum(jnp.sqrt(ss), 1e-12)

    coef = jnp.zeros((tc, LANES), f32)
    for cp in range(n_cp):
        rr, kr, vr, wlr, alr = (gather(ref, cp) for ref in (r_ref, k_ref, v_ref, wl_ref, al_ref))
        if has_vres:
            vlr, vfr = gather(vl_ref, cp), gather(vf_ref, cp)
        for par in range(2):
            c = 2 * cp + par
            r, k, v = rr[par], kr[par], vr[par]
            z = -(prow(w0_ref, c) + wlr[par])
            softplus = jnp.maximum(z, 0.0) + jnp.log1p(jnp.exp(-jnp.abs(z)))
            w = jnp.exp(-jnp.exp(-softplus - 0.5))
            a = jax.nn.sigmoid(prow(a0_ref, c) + alr[par])
            if has_vres:
                v = v + (vfr[par] - v) * jax.nn.sigmoid(prow(v0_ref, c) + vlr[par])
            kk = k * prow(kk_ref, c) / kk_den
            k2 = k * (1.0 + (a - 1.0) * prow(ka_ref, c))
            coef = coef + r * k2 * prow(rk_ref, c)
            for dst, val in ((r_t, r), (w_t, w), (k_t, k2), (v_t, v), (a_t, -kk), (b_t, kk * a)):
                dst[pl.ds(c, tc, stride=RW_HS), :] = val
            v_ct[c] = v

    def bcast_row(tile, i):
        return jnp.broadcast_to(tile[i:i + 1, :], (SUBLANES, LANES))

    def step(t, carry):
        base = pl.multiple_of(t * RW_HS, RW_HS)
        acc = [[jnp.zeros((SUBLANES, LANES), f32) for _ in range(2)] for _ in range(nvb)]
        for kb in range(nvb):
            a_tile = a_t[pl.ds(base + kb * SUBLANES, SUBLANES), :]
            for i in range(SUBLANES):
                k_i = kb * SUBLANES + i
                a_k = bcast_row(a_tile, i)
                for vb in range(nvb):
                    acc[vb][i % 2] = acc[vb][i % 2] + s_scr[vb, k_i] * a_k
        sa = [acc[vb][0] + acc[vb][1] for vb in range(nvb)]
        vt = [v_t[pl.ds(base + vb * SUBLANES, SUBLANES), :] for vb in range(nvb)]
        yacc = [[jnp.zeros((SUBLANES, LANES), f32) for _ in range(2)] for _ in range(nvb)]
        for kb in range(nvb):
            rows = pl.ds(base + kb * SUBLANES, SUBLANES)
            w_tile, b_tile, k_tile, r_tile = w_t[rows, :], b_t[rows, :], k_t[rows, :], r_t[rows, :]
            for i in range(SUBLANES):
                k_i = kb * SUBLANES + i
                w_k, b_k, k_k, r_k = (bcast_row(tl, i) for tl in (w_tile, b_tile, k_tile, r_tile))
                for vb in range(nvb):
                    s = s_scr[vb, k_i] * w_k + sa[vb] * b_k + vt[vb] * k_k
                    s_scr[vb, k_i] = s
                    yacc[vb][i % 2] = yacc[vb][i % 2] + s * r_k
        for vb in range(nvb):
            y_t[pl.ds(base + vb * SUBLANES, SUBLANES), :] = yacc[vb][0] + yacc[vb][1]
        return carry

    if t_real % tc == 0:
        lax.fori_loop(0, tc, step, 0)
    else:
        lax.fori_loop(0, jnp.clip(t_real - ci * tc, 0, tc), step, 0)

    def y_tile(c):
        return y_t[pl.ds(c, tc, stride=RW_HS), :]

    mu = jnp.zeros((tc, LANES), f32)
    for c in range(RW_HS):
        mu = mu + y_tile(c)
    mu = mu * (1.0 / RW_HS)
    var = jnp.zeros((tc, LANES), f32)
    for c in range(RW_HS):
        d = y_tile(c) - mu
        var = var + d * d
    rstd = lax.rsqrt(var * (1.0 / RW_HS) + RW_GN_EPS)
    for cp in range(n_cp):
        gates = gather(g_ref, cp)
        outs = []
        for par in range(2):
            c = 2 * cp + par
            yn = (y_tile(c) - mu) * rstd * prow(lnw_ref, c) + prow(lnb_ref, c)
            outs.append((yn + coef * v_ct[c]) * gates[par])
        o0, o1 = scatter(outs[0], outs[1])
        cols = slice(cp * LANES, (cp + 1) * LANES)
        yg_ref[0, :, cols] = o0.astype(yg_ref.dtype)
        yg_ref[1, :, cols] = o1.astype(yg_ref.dtype)

    @pl.when(ci == n_chunks - 1)
    def _():
        sout_ref[0] = s_scr[...]


def _to_ch(x, nh):
    lead = x.shape[:-1]
    return jnp.swapaxes(x.reshape(*lead, nh, RW_HS), -1, -2).reshape(*lead, nh * RW_HS)


def _state_to_tiles(s):
    b, nh = s.shape[:2]
    s = s.reshape(b // 2, 2, nh, RW_HS // SUBLANES, SUBLANES, RW_HS)
    return s.transpose(0, 3, 5, 4, 1, 2).reshape(b // 2, RW_HS // SUBLANES, RW_HS, SUBLANES, 2 * nh)


def _state_from_tiles(st, nh):
    nb = st.shape[0]
    s = st.reshape(nb, RW_HS // SUBLANES, RW_HS, SUBLANES, 2, nh).transpose(0, 4, 5, 1, 3, 2)
    return s.reshape(nb * 2, nh, RW_HS, RW_HS)


def wkv7(r, k, v, wl, al, g, vres, rows, s0, t_real):
    b, t, d = r.shape
    nh = d // RW_HS
    assert 2 * nh == LANES and b % 2 == 0
    tc = min(TILES["wkv_tc"], t)
    assert tc % SUBLANES == 0 and t % tc == 0
    n_chunks = t // tc
    assert t_real % tc == 0 or n_chunks == 1
    has_vres = vres is not None
    nvb = RW_HS // SUBLANES
    seq = pl.BlockSpec((2, tc, d), lambda bi, ci: (bi, ci, 0))
    prow = pl.BlockSpec((RW_HS, LANES), lambda bi, ci: (0, 0))
    st = pl.BlockSpec((1, nvb, RW_HS, SUBLANES, LANES), lambda bi, ci: (bi, 0, 0, 0, 0))
    seq_in = [r, k, v, wl, al, g] + (list(vres) if has_vres else [])
    names = ["w0", "a0"] + (["v0"] if has_vres else []) + ["k_k", "k_a", "r_k", "ln_w", "ln_b"]
    row_in = [jnp.tile(rows[nm].reshape(nh, RW_HS).T, (1, 2)) for nm in names]
    op_scr = pltpu.VMEM((tc * RW_HS, LANES), f32)
    yg, s_out = pl.pallas_call(
        functools.partial(_wkv_kernel, has_vres=has_vres, tc=tc, t_real=t_real, n_chunks=n_chunks),
        grid=(b // 2, n_chunks),
        in_specs=[seq] * len(seq_in) + [prow] * len(row_in) + [st],
        out_specs=[seq, st],
        out_shape=[jax.ShapeDtypeStruct((b, t, d), bf16),
                   jax.ShapeDtypeStruct((b // 2, nvb, RW_HS, SUBLANES, LANES), f32)],
        scratch_shapes=[pltpu.VMEM((nvb, RW_HS, SUBLANES, LANES), f32)] + [op_scr] * 7
        + [pltpu.VMEM((RW_HS, tc, LANES), f32)],
        compiler_params=_cp("parallel", "arbitrary"), name="wkv7",
    )(*seq_in, *row_in, _state_to_tiles(s0.astype(f32)))
    return yg, _state_from_tiles(s_out, nh)


def _diff_lambda(lam_ref, lam_init):
    lp = lam_ref[...]
    s1 = jnp.sum(lp[0:1] * lp[1:2], axis=-1, keepdims=True)
    s2 = jnp.sum(lp[2:3] * lp[3:4], axis=-1, keepdims=True)
    return jnp.exp(s1) - jnp.exp(s2) + lam_init


def _online_softmax_step(s, v_bf, m_ref, l_ref, acc_ref, c):
    m_prev = m_ref[c]
    m_new = jnp.maximum(m_prev, jnp.max(s, axis=-1, keepdims=True))
    alpha = jnp.exp(m_prev - m_new)
    p = jnp.exp(s - m_new)
    l_ref[c] = alpha * l_ref[c] + jnp.sum(p, axis=-1, keepdims=True)
    acc_ref[c] = alpha * acc_ref[c] + jnp.dot(p.astype(bf16), v_bf, preferred_element_type=f32)
    m_ref[c] = m_new


def _diff_prompt_kernel(q_ref, k_ref, v_ref, lam_ref, sub_ref, o_ref, m_ref, l_ref, acc_ref,
                        *, tq, nk, lam_init, scale):
    i, j = pl.program_id(2), pl.program_id(3)

    @pl.when(j == 0)
    def _():
        m_ref[...] = jnp.full(m_ref.shape, NEG_INF, f32)
        l_ref[...] = jnp.zeros(l_ref.shape, f32)
        acc_ref[...] = jnp.zeros(acc_ref.shape, f32)

    @pl.when(j <= i)
    def _():
        v_bf = v_ref[0].astype(bf16)
        q_pos = i * tq + lax.broadcasted_iota(jnp.int32, (tq, tq), 0)
        k_pos = j * tq + lax.broadcasted_iota(jnp.int32, (tq, tq), 1)
        for c in range(2):
            qc = q_ref[0, :, c * DA_HD:(c + 1) * DA_HD]
            kc = k_ref[0, :, c * DA_HD:(c + 1) * DA_HD].astype(bf16)
            s = lax.dot_general(qc, kc, (((1,), (1,)), ((), ())), preferred_element_type=f32) * scale
            s = jnp.where(k_pos <= q_pos, s, NEG_INF)
            _online_softmax_step(s, v_bf, m_ref, l_ref, acc_ref, c)

    @pl.when(j == nk - 1)
    def _():
        lam = _diff_lambda(lam_ref, lam_init)
        o = acc_ref[0] / l_ref[0] - lam * (acc_ref[1] / l_ref[1])
        o_ref[0] = (_rms(o, sub_ref[...]) * (1.0 - lam_init)).astype(o_ref.dtype)


def diff_attn_prompt(q, k, v, lam_params, subln, lam_init):
    b, t, d = q.shape
    hw = 2 * DA_HD
    nh = d // hw
    tq = min(TILES["attn_tq"], t)
    nq = t // tq
    qspec = pl.BlockSpec((1, tq, hw), lambda bi, hi, i, j: (bi, i, hi))
    kspec = pl.BlockSpec((1, tq, hw), lambda bi, hi, i, j: (bi, jnp.minimum(j, i), hi))
    return pl.pallas_call(
        functools.partial(_diff_prompt_kernel, tq=tq, nk=nq, lam_init=lam_init, scale=DA_HD ** -0.5),
        grid=(b, nh, nq, nq),
        in_specs=[qspec, kspec, kspec,
                  pl.BlockSpec((4, DA_HD), lambda bi, hi, i, j: (0, 0)),
                  pl.BlockSpec((1, hw), lambda bi, hi, i, j: (0, 0))],
        out_specs=qspec,
        out_shape=jax.ShapeDtypeStruct((b, t, d), bf16),
        scratch_shapes=[pltpu.VMEM((2, tq, 1), f32), pltpu.VMEM((2, tq, 1), f32), pltpu.VMEM((2, tq, hw), f32)],
        compiler_params=_cp("parallel", "parallel", "parallel", "arbitrary"), name="diff_attn_prompt",
    )(q, k, v, lam_params, subln.reshape(1, hw))


def _diff_cached_kernel(pt_ref, q_ref, kp_ref, vp_ref, kn_ref, vn_ref, lam_ref, sub_ref, o_ref,
                        m_ref, l_ref, acc_ref, *, n_pages, nh, tpad, t_real, lam_init, scale):
    p = pl.program_id(1)
    nq = nh * tpad

    @pl.when(p == 0)
    def _():
        m_ref[...] = jnp.full(m_ref.shape, NEG_INF, f32)
        l_ref[...] = jnp.zeros(l_ref.shape, f32)
        acc_ref[...] = jnp.zeros(acc_ref.shape, f32)

    def update(k3, v3, causal):
        nkeys = k3.shape[0]
        k2 = k3.reshape(nkeys * nh, 2 * DA_HD).astype(bf16)
        v2 = v3.reshape(nkeys * nh, 2 * DA_HD).astype(bf16)
        row = lax.broadcasted_iota(jnp.int32, (nq, nkeys * nh), 0)
        col = lax.broadcasted_iota(jnp.int32, (nq, nkeys * nh), 1)
        keep = (row // tpad) == (col % nh)
        if causal:
            key = col // nh
            keep = keep & (key <= row % tpad) & (key < t_real)
        ps = []
        for c in range(2):
            s = lax.dot_general(q_ref[0, c], k2[:, c * DA_HD:(c + 1) * DA_HD], (((1,), (1,)), ((), ())),
                                preferred_element_type=f32) * scale
            s = jnp.where(keep, s, NEG_INF)
            m_prev = m_ref[c]
            m_new = jnp.maximum(m_prev, jnp.max(s, axis=-1, keepdims=True))
            alpha = jnp.exp(m_prev - m_new)
            pr = jnp.exp(s - m_new)
            l_ref[c] = alpha * l_ref[c] + jnp.sum(pr, axis=-1, keepdims=True)
            acc_ref[c] = alpha * acc_ref[c]
            m_ref[c] = m_new
            ps.append(pr.astype(bf16))
        pv = jnp.dot(jnp.concatenate(ps, axis=0), v2, preferred_element_type=f32)
        acc_ref[0] += pv[:nq]
        acc_ref[1] += pv[nq:]

    @pl.when(p < n_pages)
    def _():
        update(kp_ref[...], vp_ref[...], False)

    @pl.when(p == n_pages)
    def _():
        update(kn_ref[0], vn_ref[0], True)
        lam = _diff_lambda(lam_ref, lam_init)
        o = acc_ref[0] / l_ref[0] - lam * (acc_ref[1] / l_ref[1])
        o_ref[0] = (_rms(o, sub_ref[...]) * (1.0 - lam_init)).astype(o_ref.dtype)


def diff_attn_cached(q, k_new, v_new, cache_k, cache_v, layer, page_table, lam_params, subln, lam_init, t_real):
    b, tpad, d = q.shape
    hw = 2 * DA_HD
    nh = d // hw
    n_pages = page_table.shape[1]
    page = cache_k.shape[2]
    nq = nh * tpad
    qm = q.reshape(b, tpad, nh, 2, DA_HD).transpose(0, 3, 2, 1, 4).reshape(b, 2, nq, DA_HD)
    kn = k_new.reshape(b, tpad, nh, hw)
    vn = v_new.reshape(b, tpad, nh, hw)
    last = n_pages - 1
    pspec = pl.BlockSpec((None, None, page, nh, hw),
                         lambda bi, p, pt: (layer, pt[bi, jnp.minimum(p, last)], 0, 0, 0))
    nspec = pl.BlockSpec((1, tpad, nh, hw), lambda bi, p, pt: (bi, 0, 0, 0))
    grid_spec = pltpu.PrefetchScalarGridSpec(
        num_scalar_prefetch=1,
        grid=(b, n_pages + 1),
        in_specs=[pl.BlockSpec((1, 2, nq, DA_HD), lambda bi, p, pt: (bi, 0, 0, 0)), pspec, pspec, nspec, nspec,
                  pl.BlockSpec((4, DA_HD), lambda bi, p, pt: (0, 0)),
                  pl.BlockSpec((1, hw), lambda bi, p, pt: (0, 0))],
        out_specs=pl.BlockSpec((1, nq, hw), lambda bi, p, pt: (bi, 0, 0)),
        scratch_shapes=[pltpu.VMEM((2, nq, 1), f32), pltpu.VMEM((2, nq, 1), f32), pltpu.VMEM((2, nq, hw), f32)],
    )
    o = pl.pallas_call(
        functools.partial(_diff_cached_kernel, n_pages=n_pages, nh=nh, tpad=tpad, t_real=t_real,
                          lam_init=lam_init, scale=DA_HD ** -0.5),
        grid_spec=grid_spec,
        out_shape=jax.ShapeDtypeStruct((b, nq, hw), bf16),
        compiler_params=_cp("parallel", "arbitrary"), name="diff_attn_cached",
    )(page_table, qm, cache_k, cache_v, kn, vn, lam_params, subln.reshape(1, hw))
    return o.reshape(b, nh, tpad, hw).transpose(0, 2, 1, 3).reshape(b, tpad, d)


def _mem_attn_kernel(q_ref, k_ref, v_ref, o_ref, *, hd, scale):
    for h in range(XA_HEADS):
        cols = slice(h * hd, (h + 1) * hd)
        s = lax.dot_general(q_ref[0, :, cols], k_ref[0, :, cols].astype(bf16), (((1,), (1,)), ((), ())),
                            preferred_element_type=f32) * scale
        e = jnp.exp(s - jnp.max(s, axis=-1, keepdims=True))
        p = e / jnp.sum(e, axis=-1, keepdims=True)
        o_ref[0, :, cols] = jnp.dot(p.astype(bf16), v_ref[0, :, cols].astype(bf16),
                                    preferred_element_type=f32).astype(o_ref.dtype)


def mem_attend(q, mk, mv):
    b, t, w = q.shape
    nm = mk.shape[1]
    tq = min(TILES["mem_tq"], t)
    hd = w // XA_HEADS
    qspec = pl.BlockSpec((1, tq, w), lambda bi, i: (bi, i, 0))
    mspec = pl.BlockSpec((1, nm, w), lambda bi, i: (bi, 0, 0))
    return pl.pallas_call(
        functools.partial(_mem_attn_kernel, hd=hd, scale=hd ** -0.5),
        grid=(b, t // tq),
        in_specs=[qspec, mspec, mspec],
        out_specs=qspec,
        out_shape=jax.ShapeDtypeStruct((b, t, w), bf16),
        compiler_params=_cp("parallel", "parallel"), name="mem_attend",
    )(q, mk, mv)


def _silu_gate(gate, val):
    return gate * jax.nn.sigmoid(gate) * val


def _ffn_up_short_kernel(a_ref, wg_ref, wv_ref, cwg_ref, cwv_ref, cbg_ref, cbv_ref, pg_ref, pv_ref,
                         act_ref, csg_ref, csv_ref, *, seq_len, n_sub, last_row):
    a = a_ref[...]
    row = lax.broadcasted_iota(jnp.int32, (seq_len, act_ref.shape[1]), 0)

    def half(w_ref, cw_ref, cb_ref, p_ref, cs_ref):
        h_all = jnp.dot(a, w_ref[...], preferred_element_type=f32)
        cw = cw_ref[...]
        outs = []
        for s in range(n_sub):
            h = h_all[s * seq_len:(s + 1) * seq_len]
            p0, p1 = p_ref[s, 0:1, :], p_ref[s, 1:2, :]
            r1 = jnp.where(row == 0, p1, pltpu.roll(h, 1, 0))
            r2 = jnp.where(row == 0, p0, jnp.where(row == 1, p1, pltpu.roll(h, 2, 0)))
            outs.append(cb_ref[...] + r2 * cw[0:1] + r1 * cw[1:2] + h * cw[2:3])
            cs_ref[s] = h[last_row - 1:last_row + 1]
        return jnp.concatenate(outs, axis=0) if n_sub > 1 else outs[0]

    gate = half(wg_ref, cwg_ref, cbg_ref, pg_ref, csg_ref)
    val = half(wv_ref, cwv_ref, cbv_ref, pv_ref, csv_ref)
    act_ref[...] = _silu_gate(gate, val).astype(act_ref.dtype)


def _ffn_up_long_kernel(a_ref, wg_ref, wv_ref, cwg_ref, cwv_ref, cbg_ref, cbv_ref, pg_ref, pv_ref,
                        act_ref, csg_ref, csv_ref, hbuf0, hbuf1, carry_ref,
                        *, bm, tiles_per_seq, last_row, nn):
    halo = SUBLANES
    i, j = pl.program_id(0), pl.program_id(1)

    @pl.when((i == 0) & (j == 0))
    def _():
        hbuf1[...] = jnp.zeros(hbuf1.shape, f32)

    def work(fill, drain):
        jt = jnp.maximum(j - 1, 0)
        first = (i % tiles_per_seq) == 0
        halves = ((cwg_ref, cbg_ref, pg_ref, csg_ref), (cwv_ref, cbv_ref, pv_ref, csv_ref))
        for slot, (_, _, p_ref, _) in enumerate(halves):
            @pl.when(first)
            def _():
                drain[slot, halo - 2:halo, :] = p_ref[0]

            @pl.when(jnp.logical_not(first))
            def _():
                drain[slot, halo - 2:halo, :] = carry_ref[jt, slot]

        a = a_ref[...]
        fill[0, halo:halo + bm, :] = jnp.dot(a, wg_ref[...], preferred_element_type=f32)
        fill[1, halo:halo + bm, :] = jnp.dot(a, wv_ref[...], preferred_element_type=f32)
        cvals = []
        for slot, (cw_ref, cb_ref, _, cs_ref) in enumerate(halves):
            cw = cw_ref[...]
            cvals.append(cb_ref[...] + drain[slot, halo - 2:halo - 2 + bm, :] * cw[0:1]
                         + drain[slot, halo - 1:halo - 1 + bm, :] * cw[1:2]
                         + drain[slot, halo:halo + bm, :] * cw[2:3])
            cs_ref[0] = drain[slot, halo + last_row - 1:halo + last_row + 1, :]
        act_ref[...] = _silu_gate(cvals[0], cvals[1]).astype(act_ref.dtype)

        @pl.when(j > 0)
        def _():
            for slot in range(2):
                carry_ref[jt, slot] = drain[slot, halo + bm - 2:halo + bm, :]

    @pl.when(j % 2 == 0)
    def _():
        work(hbuf0, hbuf1)

    @pl.when(j % 2 == 1)
    def _():
        work(hbuf1, hbuf0)


def ffn_up(xn, w_up, conv_w, conv_b, prev_rows, t_seq, t_real):
    m, d = xn.shape
    f2 = w_up.shape[1]
    dff = f2 // 2
    nseq = m // t_seq
    bm = _largest_tile(m, TILES["ffn_bm"], SUBLANES)
    tn = _largest_tile(dff, TILES["ffn_tn"], LANES)
    nn = dff // tn
    cw3 = conv_w.reshape(CONV_W, f2)
    cb2 = conv_b.reshape(1, f2)
    args = (xn, w_up, w_up, cw3, cw3, cb2, cb2, prev_rows, prev_rows)
    if bm >= t_seq:
        assert bm % t_seq == 0
        n_sub = bm // t_seq
        last_row = t_real - 1
        assert last_row >= CONV_W - 2
        tail_spec = pl.BlockSpec((n_sub, CONV_W - 1, tn), lambda i, j: (i, 0, j))
        tail_shape = jax.ShapeDtypeStruct((nseq, CONV_W - 1, dff), f32)
        pspec = lambda off: pl.BlockSpec((n_sub, CONV_W - 1, tn), lambda i, j: (i, 0, j + off))
        act, csg, csv = pl.pallas_call(
            functools.partial(_ffn_up_short_kernel, seq_len=t_seq, n_sub=n_sub, last_row=last_row),
            grid=(m // bm, nn),
            in_specs=[pl.BlockSpec((bm, d), lambda i, j: (i, 0)),
                      pl.BlockSpec((d, tn), lambda i, j: (0, j)),
                      pl.BlockSpec((d, tn), lambda i, j: (0, j + nn)),
                      pl.BlockSpec((CONV_W, tn), lambda i, j: (0, j)),
                      pl.BlockSpec((CONV_W, tn), lambda i, j: (0, j + nn)),
                      pl.BlockSpec((1, tn), lambda i, j: (0, j)),
                      pl.BlockSpec((1, tn), lambda i, j: (0, j + nn)),
                      pspec(0), pspec(nn)],
            out_specs=[pl.BlockSpec((bm, tn), lambda i, j: (i, j)), tail_spec, tail_spec],
            out_shape=[jax.ShapeDtypeStruct((m, dff), bf16), tail_shape, tail_shape],
            compiler_params=_cp("parallel", "parallel"), name="ffn_up_short",
        )(*args)
        return act, jnp.concatenate([csg, csv], axis=-1)

    assert t_seq % bm == 0
    tiles_per_seq = t_seq // bm
    last_tile, last_row = (t_real - 1) // bm, (t_real - 1) % bm
    assert last_tile == tiles_per_seq - 1 and last_row >= CONV_W - 2
    jm = lambda j: jnp.minimum(j, nn - 1)
    je = lambda j: jnp.maximum(j - 1, 0)
    seq_of = lambda i: i // tiles_per_seq
    tail_spec = pl.BlockSpec((1, CONV_W - 1, tn), lambda i, j: (i, 0, je(j)))
    tail_shape = jax.ShapeDtypeStruct((nseq * tiles_per_seq, CONV_W - 1, dff), f32)
    pspec = lambda off: pl.BlockSpec((1, CONV_W - 1, tn), lambda i, j: (seq_of(i), 0, je(j) + off))
    act, csg, csv = pl.pallas_call(
        functools.partial(_ffn_up_long_kernel, bm=bm, tiles_per_seq=tiles_per_seq, last_row=last_row, nn=nn),
        grid=(m // bm, nn + 1),
        in_specs=[pl.BlockSpec((bm, d), lambda i, j: (i, 0)),
                  pl.BlockSpec((d, tn), lambda i, j: (0, jm(j))),
                  pl.BlockSpec((d, tn), lambda i, j: (0, jm(j) + nn)),
                  pl.BlockSpec((CONV_W, tn), lambda i, j: (0, je(j))),
                  pl.BlockSpec((CONV_W, tn), lambda i, j: (0, je(j) + nn)),
                  pl.BlockSpec((1, tn), lambda i, j: (0, je(j))),
                  pl.BlockSpec((1, tn), lambda i, j: (0, je(j) + nn)),
                  pspec(0), pspec(nn)],
        out_specs=[pl.BlockSpec((bm, tn), lambda i, j: (i, je(j))), tail_spec, tail_spec],
        out_shape=[jax.ShapeDtypeStruct((m, dff), bf16), tail_shape, tail_shape],
        scratch_shapes=[pltpu.VMEM((2, SUBLANES + bm, tn), f32), pltpu.VMEM((2, SUBLANES + bm, tn), f32),
                        pltpu.VMEM((nn, 2, CONV_W - 1, tn), f32)],
        compiler_params=_cp("arbitrary", "arbitrary"), name="ffn_up_long",
    )(*args)
    tails = jnp.concatenate([csg, csv], axis=-1).reshape(nseq, tiles_per_seq, CONV_W - 1, f2)
    return act, tails[:, last_tile]


def _rope_tables(pos):
    half = ROT_DIM // 2
    inv_freq = ROPE_THETA ** (-jnp.arange(0, ROT_DIM, 2, dtype=f32) / ROT_DIM)
    ang = pos.astype(f32)[:, None] * inv_freq[None, :]
    cos, sin = jnp.cos(ang), jnp.sin(ang)
    n = pos.shape[0]
    z_half = jnp.zeros((n, half), f32)
    z_rest = jnp.zeros((n, DA_HD - ROT_DIM), f32)
    c = jnp.concatenate([cos, cos, jnp.ones((n, DA_HD - ROT_DIM), f32)], axis=1)
    s1 = jnp.concatenate([-sin, z_half, z_rest], axis=1)
    s2 = jnp.concatenate([z_half, sin, z_rest], axis=1)
    return c, s1, s2


def _group(x, t_real, states, prm, w, page_info):
    b, t, d = x.shape
    m = b * t
    depth = prm["norm_mix_pre"].shape[0]
    h = x.reshape(m, d)
    out = dict(k=[], v=[], wkv=[], shift=[], conv=[])
    v_first = None
    xn = None
    rope = _rope_tables(states["pos"])
    for i in range(depth):
        j = i // N_MIXERS
        if i % N_MIXERS == 0:
            mixes, last = rwkv_mix(h.reshape(b, t, d), prm["norm_mix_pre"][i], prm["rw_mix"][j],
                                   states["shift"][j], t_real)
            xr, xw, xk, xv, xa, xg = (z.reshape(m, d) for z in mixes)
            r = matmul(xr, w["rw_w_r"][j])
            k = matmul(xk, w["rw_w_k"][j])
            v = matmul(xv, w["rw_w_v"][j])
            wl = matmul(matmul(xw, w["rw_w1"][j], bf16, _epi_tanh), w["rw_w2"][j])
            al = matmul(matmul(xa, w["rw_a1"][j], bf16), w["rw_a2"][j])
            g = matmul(matmul(xg, w["rw_g1"][j], bf16, _epi_sigmoid), w["rw_g2"][j])
            rows = {nm: prm["rw_" + nm][j] for nm in ("w0", "a0", "k_k", "k_a", "ln_w", "ln_b")}
            rows["r_k"] = prm["rw_r_k"][j].reshape(d)
            if j == 0:
                vres = None
                v_first = v
            else:
                vl = matmul(matmul(xv, w["rw_v1"][j - 1], bf16), w["rw_v2"][j - 1])
                vres = (vl.reshape(b, t, d), v_first.reshape(b, t, d))
                rows["v0"] = prm["rw_v0"][j - 1]
            to3 = lambda z: z.reshape(b, t, d)
            yg, s_fin = wkv7(to3(r), to3(k), to3(v), to3(wl), to3(al), to3(g), vres, rows,
                             states["wkv"][j], t_real)
            mix = matmul(yg.reshape(m, d), w["rw_w_o"][j])
            out["wkv"].append(s_fin.astype(states["wkv"][j].dtype))
            out["shift"].append(last)
        else:
            lam_init = 0.8 - 0.6 * math.exp(-0.3 * i)
            wqkv = w["da_w_qkv"][j]
            q = matmul(xn, wqkv[0], bf16, _epi_rope, rope, t)
            k = matmul(xn, wqkv[1], f32, _epi_rope, rope, t)
            v = matmul(xn, wqkv[2])
            q3, k3, v3 = q.reshape(b, t, d), k.reshape(b, t, d), v.reshape(b, t, d)
            if page_info is None:
                o = diff_attn_prompt(q3, k3, v3, prm["da_lambda"][j], prm["da_subln"][j], lam_init)
            else:
                cache_k, cache_v, page_table = page_info
                o = diff_attn_cached(q3, k3, v3, cache_k, cache_v, j, page_table, prm["da_lambda"][j],
                                     prm["da_subln"][j], lam_init, t_real)
            mix = matmul(o.reshape(m, d), w["da_w_o"][j])
            out["k"].append(k3)
            out["v"].append(v3)
        h, xn = resid_norm(h, mix, prm["norm_mix_post"][i], prm["norm_mem_pre"][i])
        mk, mv = states["mem_k"][i], states["mem_v"][i]
        xw_ = mk.shape[-1] * mk.shape[-2]
        q = matmul(xn, w["xa_w_q"][i], bf16)
        o = mem_attend(q.reshape(b, t, xw_), mk.reshape(b, -1, xw_), mv.reshape(b, -1, xw_))
        xa_out = matmul(o.reshape(m, xw_), w["xa_w_o"][i])
        h, xn = resid_norm(h, xa_out, prm["norm_mem_post"][i], prm["norm_ffn_pre"][i])
        act, conv_state = ffn_up(xn, w["ffn_w_up"][i], prm["ffn_conv_w"][i], prm["ffn_conv_b"][i],
                                 states["conv"][i], t, t_real)
        f = matmul(act, w["ffn_w_down"][i])
        out["conv"].append(conv_state)
        nxt = i + 1
        g_next = prm["norm_mix_pre"][nxt] if (nxt < depth and nxt % N_MIXERS != 0) else None
        h, xn = resid_norm(h, f, prm["norm_ffn_post"][i], g_next)
    return h.reshape(b, t, d), out


def kernel(x_prompt, x_sample, cache_k, cache_v, state_wkv, state_shift, cache_mem_k, cache_mem_v, state_ffn_conv, page_table, mem_prompt, norm_mix_pre, norm_mix_post, norm_mem_pre, norm_mem_post, norm_ffn_pre, norm_ffn_post, mem_norm, rw_mix, rw_w_r, rw_w_k, rw_w_v, rw_w_o, rw_w0, rw_w1, rw_w2, rw_a0, rw_a1, rw_a2, rw_v0, rw_v1, rw_v2, rw_g1, rw_g2, rw_k_k, rw_k_a, rw_r_k, rw_ln_w, rw_ln_b, da_w_qkv, da_w_o, da_lambda, da_subln, xa_w_q, xa_w_kv, xa_w_o, ffn_w_up, ffn_conv_w, ffn_conv_b, ffn_w_down):
    b, t, d = x_prompt.shape
    db, ds, _ = x_sample.shape
    depth = norm_mix_pre.shape[0]
    n_rwkv = state_wkv.shape[0]
    nh_rw = d // RW_HS
    f2 = ffn_w_up.shape[-1]
    past_len = page_table.shape[1] * cache_k.shape[2]

    prm = dict(norm_mix_pre=norm_mix_pre, norm_mix_post=norm_mix_post, norm_mem_pre=norm_mem_pre,
               norm_mem_post=norm_mem_post, norm_ffn_pre=norm_ffn_pre, norm_ffn_post=norm_ffn_post,
               rw_mix=rw_mix, rw_w0=rw_w0, rw_a0=rw_a0, rw_v0=rw_v0, rw_k_k=rw_k_k, rw_k_a=rw_k_a,
               rw_r_k=rw_r_k, rw_ln_w=rw_ln_w, rw_ln_b=rw_ln_b, da_lambda=da_lambda, da_subln=da_subln,
               ffn_conv_w=ffn_conv_w, ffn_conv_b=ffn_conv_b)

    ch_out = lambda wt: _to_ch(wt, nh_rw).astype(bf16)

    def lora(w_in, w_out, permute=True):
        rank = _round_up(w_in.shape[-1], LANES)
        w_out = _to_ch(w_out, nh_rw) if permute else w_out
        return _pad_to(w_in, 2, rank).astype(bf16), _pad_to(w_out, 1, rank).astype(bf16)

    w = dict(rw_w_r=ch_out(rw_w_r), rw_w_k=ch_out(rw_w_k), rw_w_v=ch_out(rw_w_v),
             rw_w_o=jnp.swapaxes(_to_ch(jnp.swapaxes(rw_w_o, 1, 2), nh_rw), 1, 2).astype(bf16),
             da_w_o=da_w_o.astype(bf16), xa_w_q=xa_w_q.astype(bf16),
             xa_w_kv=xa_w_kv.astype(bf16), xa_w_o=xa_w_o.astype(bf16), ffn_w_up=ffn_w_up.astype(bf16),
             ffn_w_down=ffn_w_down.astype(bf16))
    w["rw_w1"], w["rw_w2"] = lora(rw_w1, rw_w2)
    w["rw_a1"], w["rw_a2"] = lora(rw_a1, rw_a2)
    w["rw_v1"], w["rw_v2"] = lora(rw_v1, rw_v2)
    w["rw_g1"], w["rw_g2"] = lora(rw_g1, rw_g2)
    w["da_w_qkv"] = [[da_w_qkv[jj, :, c * d:(c + 1) * d].astype(bf16) for c in range(3)]
                     for jj in range(da_w_qkv.shape[0])]

    mem_rows = mem_prompt.reshape(-1, d)
    mem_k, mem_v = [], []
    for i in range(depth):
        kv = matmul(rms_norm_rows(mem_rows, mem_norm[i]), w["xa_w_kv"][i])
        xw_ = kv.shape[1] // 2
        hd = xw_ // XA_HEADS
        mem_k.append(kv[:, :xw_].reshape(b, -1, XA_HEADS, hd))
        mem_v.append(kv[:, xw_:].reshape(b, -1, XA_HEADS, hd))

    states_p = dict(pos=jnp.arange(t), shift=jnp.zeros((n_rwkv, b, d), f32),
                    wkv=jnp.zeros((n_rwkv, b, nh_rw, RW_HS, RW_HS), f32), mem_k=mem_k, mem_v=mem_v,
                    conv=jnp.zeros((depth, b, CONV_W - 1, f2), f32))
    hp, out_p = _group(x_prompt, t, states_p, prm, w, None)

    tpad = _round_up(ds, SUBLANES)
    states_s = dict(pos=past_len + jnp.arange(tpad), shift=state_shift, wkv=state_wkv,
                    mem_k=cache_mem_k, mem_v=cache_mem_v, conv=state_ffn_conv)
    hs, out_s = _group(_pad_to(x_sample, 1, tpad), ds, states_s, prm, w, (cache_k, cache_v, page_table))

    nh_da = d // (2 * DA_HD)
    heads = lambda z, bb, tt: z.reshape(bb, tt, nh_da, 2 * DA_HD)
    return (hp, hs[:, :ds],
            jnp.stack([heads(z, b, t) for z in out_p["k"]]), jnp.stack([heads(z, b, t) for z in out_p["v"]]),
            jnp.stack([heads(z, db, tpad)[:, :ds] for z in out_s["k"]]),
            jnp.stack([heads(z, db, tpad)[:, :ds] for z in out_s["v"]]),
            jnp.stack(out_p["wkv"]), jnp.stack(out_s["wkv"]),
            jnp.stack(out_p["shift"]), jnp.stack(out_s["shift"]),
            jnp.stack(mem_k), jnp.stack(mem_v),
            jnp.stack(out_p["conv"]), jnp.stack(out_s["conv"]))
```

```python
import functools
import math

import jax
import jax.numpy as jnp
from jax import lax
from jax.experimental import pallas as pl
from jax.experimental.pallas import tpu as pltpu

f32 = jnp.float32
bf16 = jnp.bfloat16

NORM_EPS = 1e-6
NEG_INF = -1e30
N_MIXERS = 2
RW_HS = 64
RW_GN_EPS = 64e-5
DA_HD = 128
ROT_DIM = DA_HD // 4
ROPE_THETA = 500000.0
XA_HEADS = 4
CONV_W = 3

LANES = 128
SUBLANES = 8
VMEM_LIMIT_BYTES = 56 * 1024 * 1024

TILES = dict(
    norm_bm=128,
    mm_bm=1024, mm_bn=512, mm_bk=5504,
    mix_bt=128,
    wkv_tc=16,
    attn_tq=512,
    mem_tq=512,
    ffn_bm=1024, ffn_tn=256,
    ffn_k_chunks=8,
)


def _cp(*sem):
    return pltpu.CompilerParams(dimension_semantics=sem, vmem_limit_bytes=VMEM_LIMIT_BYTES)


def _rms(x, g):
    return x * lax.rsqrt(jnp.mean(x * x, axis=-1, keepdims=True) + NORM_EPS) * g


def _pad_to(x, axis, size):
    if x.shape[axis] == size:
        return x
    pad = [(0, 0)] * x.ndim
    pad[axis] = (0, size - x.shape[axis])
    return jnp.pad(x, pad)


def _round_up(n, m):
    return (n + m - 1) // m * m


def _norm_kernel(x_ref, g_ref, o_ref):
    o_ref[...] = _rms(x_ref[...], g_ref[...]).astype(o_ref.dtype)


def rms_norm_rows(x, g, out_dtype=bf16):
    m, d = x.shape
    bm = min(TILES["norm_bm"], m)
    return pl.pallas_call(
        _norm_kernel,
        grid=(m // bm,),
        in_specs=[pl.BlockSpec((bm, d), lambda i: (i, 0)), pl.BlockSpec((1, d), lambda i: (0, 0))],
        out_specs=pl.BlockSpec((bm, d), lambda i: (i, 0)),
        out_shape=jax.ShapeDtypeStruct((m, d), out_dtype),
        compiler_params=_cp("parallel"), name="rms_norm",
    )(x, g.reshape(1, d))


def _resid_norm_kernel(x_ref, y_ref, gp_ref, gn_ref, xo_ref, xn_ref):
    xnew = x_ref[...] + _rms(y_ref[...], gp_ref[...])
    xo_ref[...] = xnew
    xn_ref[...] = _rms(xnew, gn_ref[...]).astype(xn_ref.dtype)


def _resid_kernel(x_ref, y_ref, gp_ref, xo_ref):
    xo_ref[...] = x_ref[...] + _rms(y_ref[...], gp_ref[...])


def resid_norm(x, y, g_post, g_next=None):
    m, d = x.shape
    bm = min(TILES["norm_bm"], m)
    row = pl.BlockSpec((bm, d), lambda i: (i, 0))
    gain = pl.BlockSpec((1, d), lambda i: (0, 0))
    if g_next is None:
        return pl.pallas_call(
            _resid_kernel, grid=(m // bm,), in_specs=[row, row, gain], out_specs=row,
            out_shape=jax.ShapeDtypeStruct((m, d), f32), compiler_params=_cp("parallel"), name="resid",
        )(x, y, g_post.reshape(1, d)), None
    return pl.pallas_call(
        _resid_norm_kernel, grid=(m // bm,), in_specs=[row, row, gain, gain], out_specs=[row, row],
        out_shape=[jax.ShapeDtypeStruct((m, d), f32), jax.ShapeDtypeStruct((m, d), bf16)],
        compiler_params=_cp("parallel"), name="resid_norm",
    )(x, y, g_post.reshape(1, d), g_next.reshape(1, d))


def _epi_none(acc):
    return acc


def _epi_tanh(acc):
    return jnp.tanh(acc)


def _epi_sigmoid(acc):
    return jax.nn.sigmoid(acc)


def _epi_rope(acc, c_ref, s1_ref, s2_ref):
    c, s1, s2 = c_ref[...], s1_ref[...], s2_ref[...]
    half = ROT_DIM // 2
    outs = []
    for grp in range(acc.shape[1] // DA_HD):
        x = acc[:, grp * DA_HD:(grp + 1) * DA_HD]
        outs.append(x * c + pltpu.roll(x, DA_HD - half, 1) * s1 + pltpu.roll(x, half, 1) * s2)
    return jnp.concatenate(outs, axis=1) if len(outs) > 1 else outs[0]


def _mm_kernel(*refs, nk, epilogue, n_extra):
    a_ref, b_ref = refs[0], refs[1]
    extra = refs[2:2 + n_extra]
    o_ref = refs[2 + n_extra]
    part = jnp.dot(a_ref[...], b_ref[...], preferred_element_type=f32)
    if nk == 1:
        o_ref[...] = epilogue(part, *extra).astype(o_ref.dtype)
        return
    acc_ref = refs[3 + n_extra]
    k = pl.program_id(2)

    @pl.when(k == 0)
    def _():
        acc_ref[...] = part

    @pl.when(k > 0)
    def _():
        acc_ref[...] += part

    @pl.when(k == nk - 1)
    def _():
        o_ref[...] = epilogue(acc_ref[...], *extra).astype(o_ref.dtype)


def _largest_tile(n, cap, unit):
    if n <= cap:
        return n
    best = None
    for t in range(unit, cap + 1, unit):
        if n % t == 0:
            best = t
    assert best is not None, (n, cap, unit)
    return best


def matmul(a, b, out_dtype=f32, epilogue=_epi_none, row_tables=(), table_rows=None):
    m, k = a.shape
    k2, n = b.shape
    assert k == k2 and a.dtype == bf16 and b.dtype == bf16
    bm = _largest_tile(m, TILES["mm_bm"], SUBLANES)
    bn = _largest_tile(n, TILES["mm_bn"], LANES)
    bk = _largest_tile(k, TILES["mm_bk"], LANES)
    nk = k // bk
    in_specs = [pl.BlockSpec((bm, bk), lambda i, j, kk: (i, kk)),
                pl.BlockSpec((bk, bn), lambda i, j, kk: (kk, j))]
    tables = []
    for t in row_tables:
        if table_rows < bm:
            assert bm % table_rows == 0
            t = jnp.tile(t, (bm // table_rows, 1))
        assert t.shape[0] % bm == 0
        nrep = t.shape[0] // bm
        tables.append(t)
        in_specs.append(pl.BlockSpec((bm, t.shape[1]), lambda i, j, kk, nrep=nrep: (i % nrep, 0)))
    scratch = [pltpu.VMEM((bm, bn), f32)] if nk > 1 else []
    return pl.pallas_call(
        functools.partial(_mm_kernel, nk=nk, epilogue=epilogue, n_extra=len(tables)),
        grid=(m // bm, n // bn, nk),
        in_specs=in_specs,
        out_specs=pl.BlockSpec((bm, bn), lambda i, j, kk: (i, j)),
        out_shape=jax.ShapeDtypeStruct((m, n), out_dtype),
        scratch_shapes=scratch,
        compiler_params=_cp("parallel", "parallel", "arbitrary"), name="matmul_" + epilogue.__name__[5:],
    )(a, b, *tables)


def _mix_kernel(h_ref, g_ref, mix_ref, prev_ref, *refs, bt, last_tile, last_row):
    outs, last_ref, carry_ref = refs[:6], refs[6], refs[7]
    tj = pl.program_id(1)
    xn = _rms(h_ref[0], g_ref[...])

    @pl.when(tj == 0)
    def _():
        carry_ref[SUBLANES - 1:SUBLANES, :] = prev_ref[0]

    prev_row = carry_ref[SUBLANES - 1:SUBLANES, :]
    row = lax.broadcasted_iota(jnp.int32, xn.shape, 0)
    shifted = jnp.where(row == 0, prev_row, pltpu.roll(xn, 1, 0))
    xx = shifted - xn
    for m_i in range(6):
        outs[m_i][0] = (xn + xx * mix_ref[m_i:m_i + 1, :]).astype(outs[m_i].dtype)
    carry_ref[...] = xn[bt - SUBLANES:bt, :]

    @pl.when(tj == last_tile)
    def _():
        last_ref[0] = xn[last_row:last_row + 1, :]


def rwkv_mix(h, g, mix, x_prev, t_real):
    b, t, d = h.shape
    bt = min(TILES["mix_bt"], t)
    seq = pl.BlockSpec((1, bt, d), lambda bi, tj: (bi, tj, 0))
    one = pl.BlockSpec((1, 1, d), lambda bi, tj: (bi, 0, 0))
    outs = pl.pallas_call(
        functools.partial(_mix_kernel, bt=bt, last_tile=(t_real - 1) // bt, last_row=(t_real - 1) % bt),
        grid=(b, t // bt),
        in_specs=[seq, pl.BlockSpec((1, d), lambda bi, tj: (0, 0)),
                  pl.BlockSpec((6, d), lambda bi, tj: (0, 0)), one],
        out_specs=[seq] * 6 + [one],
        out_shape=[jax.ShapeDtypeStruct((b, t, d), bf16)] * 6 + [jax.ShapeDtypeStruct((b, 1, d), f32)],
        scratch_shapes=[pltpu.VMEM((SUBLANES, d), f32)],
        compiler_params=_cp("parallel", "arbitrary"), name="rwkv_mix",
    )(h, g.reshape(1, d), mix, x_prev.reshape(b, 1, d))
    return outs[:6], outs[6].reshape(b, d)


def _wkv_kernel(*refs, has_vres, tc, t_real, n_chunks):
    it = iter(refs)
    r_ref, k_ref, v_ref, wl_ref, al_ref, g_ref = (next(it) for _ in range(6))
    vl_ref, vf_ref = (next(it), next(it)) if has_vres else (None, None)
    w0_ref, a0_ref = next(it), next(it)
    v0_ref = next(it) if has_vres else None
    kk_ref, ka_ref, rk_ref, lnw_ref, lnb_ref, s0_ref = (next(it) for _ in range(6))
    yg_ref, sout_ref = next(it), next(it)
    s_scr = next(it)
    r_t, w_t, k_t, v_t, a_t, b_t, y_t = (next(it) for _ in range(7))
    ops_ct = [next(it) for _ in range(6)]
    v_ct = ops_ct[3]

    nvb = RW_HS // SUBLANES
    n_cp = RW_HS // 2
    ci = pl.program_id(1)

    @pl.when(ci == 0)
    def _():
        s_scr[...] = s0_ref[0]

    y_t[...] = jnp.zeros(y_t.shape, f32)

    low = lax.broadcasted_iota(jnp.int32, (tc, LANES), 1) < RW_HS

    def gather(ref, cp):
        cols = slice(cp * LANES, (cp + 1) * LANES)
        g0, g1 = ref[0, :, cols], ref[1, :, cols]
        return (jnp.where(low, g0, pltpu.roll(g1, RW_HS, 1)), jnp.where(low, pltpu.roll(g0, RW_HS, 1), g1))

    def scatter(x_even, x_odd):
        return (jnp.where(low, x_even, pltpu.roll(x_odd, RW_HS, 1)), jnp.where(low, pltpu.roll(x_even, RW_HS, 1), x_odd))

    def prow(ref, c):
        return ref[c:c + 1, :]

    ss = jnp.zeros((tc, LANES), f32)
    for cp in range(n_cp):
        for par, kc in enumerate(gather(k_ref, cp)):
            kk = kc * prow(kk_ref, 2 * cp + par)
            ss = ss + kk * kk
    kk_den = jnp.maximum(jnp.sqrt(ss), 1e-12)

    coef = jnp.zeros((tc, LANES), f32)
    for cp in range(n_cp):
        rr, kr, vr, wlr, alr = (gather(ref, cp) for ref in (r_ref, k_ref, v_ref, wl_ref, al_ref))
        if has_vres:
            vlr, vfr = gather(vl_ref, cp), gather(vf_ref, cp)
        for par in range(2):
            c = 2 * cp + par
            r, k, v = rr[par], kr[par], vr[par]
            z = -(prow(w0_ref, c) + wlr[par])
            softplus = jnp.maximum(z, 0.0) + jnp.log1p(jnp.exp(-jnp.abs(z)))
            w = jnp.exp(-jnp.exp(-softplus - 0.5))
            a = jax.nn.sigmoid(prow(a0_ref, c) + alr[par])
            if has_vres:
                v = v + (vfr[par] - v) * jax.nn.sigmoid(prow(v0_ref, c) + vlr[par])
            kk = k * prow(kk_ref, c) / kk_den
            k2 = k * (1.0 + (a - 1.0) * prow(ka_ref, c))
            coef = coef + r * k2 * prow(rk_ref, c)
            for dst, val in zip(ops_ct, (r, w, k2, v, -kk, kk * a)):
                dst[c * tc:(c + 1) * tc, :] = val

    for src, dst in zip(ops_ct, (r_t, w_t, k_t, v_t, a_t, b_t)):
        for tt in range(tc):
            dst[tt * RW_HS:(tt + 1) * RW_HS, :] = src[pl.ds(tt, RW_HS, stride=tc), :]

    def bcast_row(tile, i):
        return jnp.broadcast_to(tile[i:i + 1, :], (SUBLANES, LANES))

    def step(t, carry):
        base = pl.multiple_of(t * RW_HS, RW_HS)
        acc = [[jnp.zeros((SUBLANES, LANES), f32) for _ in range(2)] for _ in range(nvb)]
        for kb in range(nvb):
            a_tile = a_t[pl.ds(base + kb * SUBLANES, SUBLANES), :]
            for i in range(SUBLANES):
                k_i = kb * SUBLANES + i
                a_k = bcast_row(a_tile, i)
                for vb in range(nvb):
                    acc[vb][i % 2] = acc[vb][i % 2] + s_scr[vb, k_i] * a_k
        sa = [acc[vb][0] + acc[vb][1] for vb in range(nvb)]
        vt = [v_t[pl.ds(base + vb * SUBLANES, SUBLANES), :] for vb in range(nvb)]
        yacc = [[jnp.zeros((SUBLANES, LANES), f32) for _ in range(2)] for _ in range(nvb)]
        for kb in range(nvb):
            rows = pl.ds(base + kb * SUBLANES, SUBLANES)
            w_tile, b_tile, k_tile, r_tile = w_t[rows, :], b_t[rows, :], k_t[rows, :], r_t[rows, :]
            for i in range(SUBLANES):
                k_i = kb * SUBLANES + i
                w_k, b_k, k_k, r_k = (bcast_row(tl, i) for tl in (w_tile, b_tile, k_tile, r_tile))
                for vb in range(nvb):
                    s = s_scr[vb, k_i] * w_k + sa[vb] * b_k + vt[vb] * k_k
                    s_scr[vb, k_i] = s
                    yacc[vb][i % 2] = yacc[vb][i % 2] + s * r_k
        for vb in range(nvb):
            y_t[pl.ds(base + vb * SUBLANES, SUBLANES), :] = yacc[vb][0] + yacc[vb][1]
        return carry

    if t_real % tc == 0:
        lax.fori_loop(0, tc, step, 0)
    else:
        lax.fori_loop(0, jnp.clip(t_real - ci * tc, 0, tc), step, 0)

    def y_tile(c):
        return y_t[pl.ds(c, tc, stride=RW_HS), :]

    mu = jnp.zeros((tc, LANES), f32)
    for c in range(RW_HS):
        mu = mu + y_tile(c)
    mu = mu * (1.0 / RW_HS)
    var = jnp.zeros((tc, LANES), f32)
    for c in range(RW_HS):
        d = y_tile(c) - mu
        var = var + d * d
    rstd = lax.rsqrt(var * (1.0 / RW_HS) + RW_GN_EPS)
    for cp in range(n_cp):
        gates = gather(g_ref, cp)
        outs = []
        for par in range(2):
            c = 2 * cp + par
            yn = (y_tile(c) - mu) * rstd * prow(lnw_ref, c) + prow(lnb_ref, c)
            outs.append((yn + coef * v_ct[c * tc:(c + 1) * tc, :]) * gates[par])
        o0, o1 = scatter(outs[0], outs[1])
        cols = slice(cp * LANES, (cp + 1) * LANES)
        yg_ref[0, :, cols] = o0.astype(yg_ref.dtype)
        yg_ref[1, :, cols] = o1.astype(yg_ref.dtype)

    @pl.when(ci == n_chunks - 1)
    def _():
        sout_ref[0] = s_scr[...]


def _to_ch(x, nh):
    lead = x.shape[:-1]
    return jnp.swapaxes(x.reshape(*lead, nh, RW_HS), -1, -2).reshape(*lead, nh * RW_HS)


def _state_to_tiles(s):
    b, nh = s.shape[:2]
    s = s.reshape(b // 2, 2, nh, RW_HS // SUBLANES, SUBLANES, RW_HS)
    return s.transpose(0, 3, 5, 4, 1, 2).reshape(b // 2, RW_HS // SUBLANES, RW_HS, SUBLANES, 2 * nh)


def _state_from_tiles(st, nh):
    nb = st.shape[0]
    s = st.reshape(nb, RW_HS // SUBLANES, RW_HS, SUBLANES, 2, nh).transpose(0, 4, 5, 1, 3, 2)
    return s.reshape(nb * 2, nh, RW_HS, RW_HS)


def wkv7(r, k, v, wl, al, g, vres, rows, s0, t_real):
    b, t, d = r.shape
    nh = d // RW_HS
    assert 2 * nh == LANES and b % 2 == 0
    tc = min(TILES["wkv_tc"], t)
    assert tc % SUBLANES == 0 and t % tc == 0
    n_chunks = t // tc
    assert t_real % tc == 0 or n_chunks == 1
    has_vres = vres is not None
    nvb = RW_HS // SUBLANES
    seq = pl.BlockSpec((2, tc, d), lambda bi, ci: (bi, ci, 0))
    prow = pl.BlockSpec((RW_HS, LANES), lambda bi, ci: (0, 0))
    st = pl.BlockSpec((1, nvb, RW_HS, SUBLANES, LANES), lambda bi, ci: (bi, 0, 0, 0, 0))
    seq_in = [r, k, v, wl, al, g] + (list(vres) if has_vres else [])
    names = ["w0", "a0"] + (["v0"] if has_vres else []) + ["k_k", "k_a", "r_k", "ln_w", "ln_b"]
    row_in = [jnp.tile(rows[nm].reshape(nh, RW_HS).T, (1, 2)) for nm in names]
    op_scr = pltpu.VMEM((tc * RW_HS, LANES), f32)
    yg, s_out = pl.pallas_call(
        functools.partial(_wkv_kernel, has_vres=has_vres, tc=tc, t_real=t_real, n_chunks=n_chunks),
        grid=(b // 2, n_chunks),
        in_specs=[seq] * len(seq_in) + [prow] * len(row_in) + [st],
        out_specs=[seq, st],
        out_shape=[jax.ShapeDtypeStruct((b, t, d), bf16),
                   jax.ShapeDtypeStruct((b // 2, nvb, RW_HS, SUBLANES, LANES), f32)],
        scratch_shapes=[pltpu.VMEM((nvb, RW_HS, SUBLANES, LANES), f32)] + [op_scr] * 13,
        compiler_params=_cp("parallel", "arbitrary"), name="wkv7",
    )(*seq_in, *row_in, _state_to_tiles(s0.astype(f32)))
    return yg, _state_from_tiles(s_out, nh)


def _diff_lambda(lam_ref, lam_init):
    lp = lam_ref[...]
    s1 = jnp.sum(lp[0:1] * lp[1:2], axis=-1, keepdims=True)
    s2 = jnp.sum(lp[2:3] * lp[3:4], axis=-1, keepdims=True)
    return jnp.exp(s1) - jnp.exp(s2) + lam_init


def _online_softmax_step(s, v_bf, m_ref, l_ref, acc_ref, c):
    m_prev = m_ref[c]
    m_new = jnp.maximum(m_prev, jnp.max(s, axis=-1, keepdims=True))
    alpha = jnp.exp(m_prev - m_new)
    p = jnp.exp(s - m_new)
    l_ref[c] = alpha * l_ref[c] + jnp.sum(p, axis=-1, keepdims=True)
    acc_ref[c] = alpha * acc_ref[c] + jnp.dot(p.astype(bf16), v_bf, preferred_element_type=f32)
    m_ref[c] = m_new


def _diff_prompt_kernel(q_ref, k_ref, v_ref, lam_ref, sub_ref, o_ref, m_ref, l_ref, acc_ref,
                        *, tq, nk, lam_init, scale):
    i, j = pl.program_id(2), pl.program_id(3)

    @pl.when(j == 0)
    def _():
        m_ref[...] = jnp.full(m_ref.shape, NEG_INF, f32)
        l_ref[...] = jnp.zeros(l_ref.shape, f32)
        acc_ref[...] = jnp.zeros(acc_ref.shape, f32)

    def block(diagonal):
        v_bf = v_ref[0].astype(bf16)
        for c in range(2):
            qc = q_ref[0, :, c * DA_HD:(c + 1) * DA_HD]
            kc = k_ref[0, :, c * DA_HD:(c + 1) * DA_HD].astype(bf16)
            s = lax.dot_general(qc, kc, (((1,), (1,)), ((), ())), preferred_element_type=f32) * scale
            if diagonal:
                q_pos = lax.broadcasted_iota(jnp.int32, (tq, tq), 0)
                k_pos = lax.broadcasted_iota(jnp.int32, (tq, tq), 1)
                s = jnp.where(k_pos <= q_pos, s, NEG_INF)
            _online_softmax_step(s, v_bf, m_ref, l_ref, acc_ref, c)

    @pl.when(j < i)
    def _():
        block(False)

    @pl.when(j == i)
    def _():
        block(True)

    @pl.when(j == nk - 1)
    def _():
        lam = _diff_lambda(lam_ref, lam_init)
        o = acc_ref[0] / l_ref[0] - lam * (acc_ref[1] / l_ref[1])
        o_ref[0] = (_rms(o, sub_ref[...]) * (1.0 - lam_init)).astype(o_ref.dtype)


def diff_attn_prompt(q, k, v, lam_params, subln, lam_init):
    b, t, d = q.shape
    hw = 2 * DA_HD
    nh = d // hw
    tq = min(TILES["attn_tq"], t)
    nq = t // tq
    qspec = pl.BlockSpec((1, tq, hw), lambda bi, hi, i, j: (bi, i, hi))
    kspec = pl.BlockSpec((1, tq, hw), lambda bi, hi, i, j: (bi, jnp.minimum(j, i), hi))
    return pl.pallas_call(
        functools.partial(_diff_prompt_kernel, tq=tq, nk=nq, lam_init=lam_init, scale=DA_HD ** -0.5),
        grid=(b, nh, nq, nq),
        in_specs=[qspec, kspec, kspec,
                  pl.BlockSpec((4, DA_HD), lambda bi, hi, i, j: (0, 0)),
                  pl.BlockSpec((1, hw), lambda bi, hi, i, j: (0, 0))],
        out_specs=qspec,
        out_shape=jax.ShapeDtypeStruct((b, t, d), bf16),
        scratch_shapes=[pltpu.VMEM((2, tq, 1), f32), pltpu.VMEM((2, tq, 1), f32), pltpu.VMEM((2, tq, hw), f32)],
        compiler_params=_cp("parallel", "parallel", "parallel", "arbitrary"), name="diff_attn_prompt",
    )(q, k, v, lam_params, subln.reshape(1, hw))


def _diff_cached_kernel(pt_ref, q_ref, kp_ref, vp_ref, kn_ref, vn_ref, lam_ref, sub_ref, o_ref,
                        m_ref, l_ref, acc_ref, *, n_pages, nh, tpad, t_real, lam_init, scale):
    p = pl.program_id(1)
    nq = nh * tpad

    @pl.when(p == 0)
    def _():
        m_ref[...] = jnp.full(m_ref.shape, NEG_INF, f32)
        l_ref[...] = jnp.zeros(l_ref.shape, f32)
        acc_ref[...] = jnp.zeros(acc_ref.shape, f32)

    def update(k3, v3, causal):
        nkeys = k3.shape[0]
        k2 = k3.reshape(nkeys * nh, 2 * DA_HD).astype(bf16)
        v2 = v3.reshape(nkeys * nh, 2 * DA_HD).astype(bf16)
        row = lax.broadcasted_iota(jnp.int32, (nq, nkeys * nh), 0)
        col = lax.broadcasted_iota(jnp.int32, (nq, nkeys * nh), 1)
        keep = (row // tpad) == (col % nh)
        if causal:
            key = col // nh
            keep = keep & (key <= row % tpad) & (key < t_real)
        ps = []
        for c in range(2):
            s = lax.dot_general(q_ref[0, c], k2[:, c * DA_HD:(c + 1) * DA_HD], (((1,), (1,)), ((), ())),
                                preferred_element_type=f32) * scale
            s = jnp.where(keep, s, NEG_INF)
            m_prev = m_ref[c]
            m_new = jnp.maximum(m_prev, jnp.max(s, axis=-1, keepdims=True))
            alpha = jnp.exp(m_prev - m_new)
            pr = jnp.exp(s - m_new)
            l_ref[c] = alpha * l_ref[c] + jnp.sum(pr, axis=-1, keepdims=True)
            acc_ref[c] = alpha * acc_ref[c]
            m_ref[c] = m_new
            ps.append(pr.astype(bf16))
        pv = jnp.dot(jnp.concatenate(ps, axis=0), v2, preferred_element_type=f32)
        acc_ref[0] += pv[:nq]
        acc_ref[1] += pv[nq:]

    @pl.when(p < n_pages)
    def _():
        update(kp_ref[...], vp_ref[...], False)

    @pl.when(p == n_pages)
    def _():
        update(kn_ref[0], vn_ref[0], True)
        lam = _diff_lambda(lam_ref, lam_init)
        o = acc_ref[0] / l_ref[0] - lam * (acc_ref[1] / l_ref[1])
        o_ref[0] = (_rms(o, sub_ref[...]) * (1.0 - lam_init)).astype(o_ref.dtype)


def diff_attn_cached(q, k_new, v_new, cache_k, cache_v, layer, page_table, lam_params, subln, lam_init, t_real):
    b, tpad, d = q.shape
    hw = 2 * DA_HD
    nh = d // hw
    n_pages = page_table.shape[1]
    page = cache_k.shape[2]
    nq = nh * tpad
    qm = q.reshape(b, tpad, nh, 2, DA_HD).transpose(0, 3, 2, 1, 4).reshape(b, 2, nq, DA_HD)
    kn = k_new.reshape(b, tpad, nh, hw)
    vn = v_new.reshape(b, tpad, nh, hw)
    last = n_pages - 1
    pspec = pl.BlockSpec((None, None, page, nh, hw),
                         lambda bi, p, pt: (layer, pt[bi, jnp.minimum(p, last)], 0, 0, 0))
    nspec = pl.BlockSpec((1, tpad, nh, hw), lambda bi, p, pt: (bi, 0, 0, 0))
    grid_spec = pltpu.PrefetchScalarGridSpec(
        num_scalar_prefetch=1,
        grid=(b, n_pages + 1),
        in_specs=[pl.BlockSpec((1, 2, nq, DA_HD), lambda bi, p, pt: (bi, 0, 0, 0)), pspec, pspec, nspec, nspec,
                  pl.BlockSpec((4, DA_HD), lambda bi, p, pt: (0, 0)),
                  pl.BlockSpec((1, hw), lambda bi, p, pt: (0, 0))],
        out_specs=pl.BlockSpec((1, nq, hw), lambda bi, p, pt: (bi, 0, 0)),
        scratch_shapes=[pltpu.VMEM((2, nq, 1), f32), pltpu.VMEM((2, nq, 1), f32), pltpu.VMEM((2, nq, hw), f32)],
    )
    o = pl.pallas_call(
        functools.partial(_diff_cached_kernel, n_pages=n_pages, nh=nh, tpad=tpad, t_real=t_real,
                          lam_init=lam_init, scale=DA_HD ** -0.5),
        grid_spec=grid_spec,
        out_shape=jax.ShapeDtypeStruct((b, nq, hw), bf16),
        compiler_params=_cp("parallel", "arbitrary"), name="diff_attn_cached",
    )(page_table, qm, cache_k, cache_v, kn, vn, lam_params, subln.reshape(1, hw))
    return o.reshape(b, nh, tpad, hw).transpose(0, 2, 1, 3).reshape(b, tpad, d)


def _mem_attn_kernel(q_ref, k_ref, v_ref, o_ref, *, hd, scale):
    for h in range(XA_HEADS):
        cols = slice(h * hd, (h + 1) * hd)
        s = lax.dot_general(q_ref[0, :, cols], k_ref[0, :, cols].astype(bf16), (((1,), (1,)), ((), ())),
                            preferred_element_type=f32) * scale
        e = jnp.exp(s - jnp.max(s, axis=-1, keepdims=True))
        p = e / jnp.sum(e, axis=-1, keepdims=True)
        o_ref[0, :, cols] = jnp.dot(p.astype(bf16), v_ref[0, :, cols].astype(bf16),
                                    preferred_element_type=f32).astype(o_ref.dtype)


def mem_attend(q, mk, mv):
    b, t, w = q.shape
    nm = mk.shape[1]
    tq = min(TILES["mem_tq"], t)
    hd = w // XA_HEADS
    qspec = pl.BlockSpec((1, tq, w), lambda bi, i: (bi, i, 0))
    mspec = pl.BlockSpec((1, nm, w), lambda bi, i: (bi, 0, 0))
    return pl.pallas_call(
        functools.partial(_mem_attn_kernel, hd=hd, scale=hd ** -0.5),
        grid=(b, t // tq),
        in_specs=[qspec, mspec, mspec],
        out_specs=qspec,
        out_shape=jax.ShapeDtypeStruct((b, t, w), bf16),
        compiler_params=_cp("parallel", "parallel"), name="mem_attend",
    )(q, mk, mv)


def _silu_gate(gate, val):
    return gate * jax.nn.sigmoid(gate) * val


def _ffn_up_short_kernel(a_ref, wg_ref, wv_ref, cwg_ref, cwv_ref, cbg_ref, cbv_ref, pg_ref, pv_ref,
                         act_ref, csg_ref, csv_ref, *, seq_len, n_sub, last_row):
    a = a_ref[...]
    row = lax.broadcasted_iota(jnp.int32, (seq_len, act_ref.shape[1]), 0)

    def half(w_ref, cw_ref, cb_ref, p_ref, cs_ref):
        h_all = jnp.dot(a, w_ref[...], preferred_element_type=f32)
        cw = cw_ref[...]
        outs = []
        for s in range(n_sub):
            h = h_all[s * seq_len:(s + 1) * seq_len]
            p0, p1 = p_ref[s, 0:1, :], p_ref[s, 1:2, :]
            r1 = jnp.where(row == 0, p1, pltpu.roll(h, 1, 0))
            r2 = jnp.where(row == 0, p0, jnp.where(row == 1, p1, pltpu.roll(h, 2, 0)))
            outs.append(cb_ref[...] + r2 * cw[0:1] + r1 * cw[1:2] + h * cw[2:3])
            cs_ref[s] = h[last_row - 1:last_row + 1]
        return jnp.concatenate(outs, axis=0) if n_sub > 1 else outs[0]

    gate = half(wg_ref, cwg_ref, cbg_ref, pg_ref, csg_ref)
    val = half(wv_ref, cwv_ref, cbv_ref, pv_ref, csv_ref)
    act_ref[...] = _silu_gate(gate, val).astype(act_ref.dtype)


def _ffn_up_long_kernel(a_ref, wg_ref, wv_ref, cwg_ref, cwv_ref, cbg_ref, cbv_ref, pg_ref, pv_ref,
                        act_ref, csg_ref, csv_ref, hbuf0, hbuf1, carry_ref,
                        *, bm, tiles_per_seq, last_row, nn, k_chunks):
    halo = SUBLANES
    i, j = pl.program_id(0), pl.program_id(1)

    @pl.when((i == 0) & (j == 0))
    def _():
        hbuf1[...] = jnp.zeros(hbuf1.shape, f32)

    def work(fill, drain):
        jt = jnp.maximum(j - 1, 0)
        first = (i % tiles_per_seq) == 0
        halves = ((cwg_ref, cbg_ref, pg_ref, csg_ref), (cwv_ref, cbv_ref, pv_ref, csv_ref))
        for slot, (_, _, p_ref, _) in enumerate(halves):
            @pl.when(first)
            def _():
                drain[slot, halo - 2:halo, :] = p_ref[0]

            @pl.when(jnp.logical_not(first))
            def _():
                drain[slot, halo - 2:halo, :] = carry_ref[jt, slot]

        rc = min(bm, 64)

        def gate_rows(r0):
            cvals = []
            for slot, (cw_ref, cb_ref, _, _) in enumerate(halves):
                cw = cw_ref[...]
                cvals.append(cb_ref[...] + drain[slot, halo - 2 + r0:halo - 2 + r0 + rc, :] * cw[0:1]
                             + drain[slot, halo - 1 + r0:halo - 1 + r0 + rc, :] * cw[1:2]
                             + drain[slot, halo + r0:halo + r0 + rc, :] * cw[2:3])
            act_ref[r0:r0 + rc, :] = _silu_gate(cvals[0], cvals[1]).astype(act_ref.dtype)

        segs = [(slot, kq) for kq in range(k_chunks) for slot in range(2)]
        row_chunks = list(range(0, bm, rc))
        per_seg = -(-len(row_chunks) // len(segs))
        kc = a_ref.shape[1] // k_chunks
        for si, (slot, kq) in enumerate(segs):
            w_ref = (wg_ref, wv_ref)[slot]
            part = jnp.dot(a_ref[:, kq * kc:(kq + 1) * kc], w_ref[kq * kc:(kq + 1) * kc, :],
                           preferred_element_type=f32)
            if kq == 0:
                fill[slot, halo:halo + bm, :] = part
            else:
                fill[slot, halo:halo + bm, :] += part
            for r0 in row_chunks[si * per_seg:(si + 1) * per_seg]:
                gate_rows(r0)
        for slot, (_, _, _, cs_ref) in enumerate(halves):
            cs_ref[0] = drain[slot, halo + last_row - 1:halo + last_row + 1, :]

        @pl.when(j > 0)
        def _():
            for slot in range(2):
                carry_ref[jt, slot] = drain[slot, halo + bm - 2:halo + bm, :]

    @pl.when(j % 2 == 0)
    def _():
        work(hbuf0, hbuf1)

    @pl.when(j % 2 == 1)
    def _():
        work(hbuf1, hbuf0)


def ffn_up(xn, w_up, conv_w, conv_b, prev_rows, t_seq, t_real):
    m, d = xn.shape
    f2 = w_up.shape[1]
    dff = f2 // 2
    nseq = m // t_seq
    bm = _largest_tile(m, TILES["ffn_bm"], SUBLANES)
    tn = _largest_tile(dff, TILES["ffn_tn"], LANES)
    nn = dff // tn
    cw3 = conv_w.reshape(CONV_W, f2)
    cb2 = conv_b.reshape(1, f2)
    args = (xn, w_up, w_up, cw3, cw3, cb2, cb2, prev_rows, prev_rows)
    if bm >= t_seq:
        assert bm % t_seq == 0
        n_sub = bm // t_seq
        last_row = t_real - 1
        assert last_row >= CONV_W - 2
        tail_spec = pl.BlockSpec((n_sub, CONV_W - 1, tn), lambda i, j: (i, 0, j))
        tail_shape = jax.ShapeDtypeStruct((nseq, CONV_W - 1, dff), f32)
        pspec = lambda off: pl.BlockSpec((n_sub, CONV_W - 1, tn), lambda i, j: (i, 0, j + off))
        act, csg, csv = pl.pallas_call(
            functools.partial(_ffn_up_short_kernel, seq_len=t_seq, n_sub=n_sub, last_row=last_row),
            grid=(m // bm, nn),
            in_specs=[pl.BlockSpec((bm, d), lambda i, j: (i, 0)),
                      pl.BlockSpec((d, tn), lambda i, j: (0, j)),
                      pl.BlockSpec((d, tn), lambda i, j: (0, j + nn)),
                      pl.BlockSpec((CONV_W, tn), lambda i, j: (0, j)),
                      pl.BlockSpec((CONV_W, tn), lambda i, j: (0, j + nn)),
                      pl.BlockSpec((1, tn), lambda i, j: (0, j)),
                      pl.BlockSpec((1, tn), lambda i, j: (0, j + nn)),
                      pspec(0), pspec(nn)],
            out_specs=[pl.BlockSpec((bm, tn), lambda i, j: (i, j)), tail_spec, tail_spec],
            out_shape=[jax.ShapeDtypeStruct((m, dff), bf16), tail_shape, tail_shape],
            compiler_params=_cp("parallel", "parallel"), name="ffn_up_short",
        )(*args)
        return act, jnp.concatenate([csg, csv], axis=-1)

    assert t_seq % bm == 0
    tiles_per_seq = t_seq // bm
    last_tile, last_row = (t_real - 1) // bm, (t_real - 1) % bm
    assert last_tile == tiles_per_seq - 1 and last_row >= CONV_W - 2
    jm = lambda j: jnp.minimum(j, nn - 1)
    je = lambda j: jnp.maximum(j - 1, 0)
    seq_of = lambda i: i // tiles_per_seq
    tail_spec = pl.BlockSpec((1, CONV_W - 1, tn), lambda i, j: (i, 0, je(j)))
    tail_shape = jax.ShapeDtypeStruct((nseq * tiles_per_seq, CONV_W - 1, dff), f32)
    pspec = lambda off: pl.BlockSpec((1, CONV_W - 1, tn), lambda i, j: (seq_of(i), 0, je(j) + off))
    act, csg, csv = pl.pallas_call(
        functools.partial(_ffn_up_long_kernel, bm=bm, tiles_per_seq=tiles_per_seq, last_row=last_row, nn=nn,
                          k_chunks=TILES["ffn_k_chunks"] if d % (TILES["ffn_k_chunks"] * 2 * LANES) == 0 else 1),
        grid=(m // bm, nn + 1),
        in_specs=[pl.BlockSpec((bm, d), lambda i, j: (i, 0)),
                  pl.BlockSpec((d, tn), lambda i, j: (0, jm(j))),
                  pl.BlockSpec((d, tn), lambda i, j: (0, jm(j) + nn)),
                  pl.BlockSpec((CONV_W, tn), lambda i, j: (0, je(j))),
                  pl.BlockSpec((CONV_W, tn), lambda i, j: (0, je(j) + nn)),
                  pl.BlockSpec((1, tn), lambda i, j: (0, je(j))),
                  pl.BlockSpec((1, tn), lambda i, j: (0, je(j) + nn)),
                  pspec(0), pspec(nn)],
        out_specs=[pl.BlockSpec((bm, tn), lambda i, j: (i, je(j))), tail_spec, tail_spec],
        out_shape=[jax.ShapeDtypeStruct((m, dff), bf16), tail_shape, tail_shape],
        scratch_shapes=[pltpu.VMEM((2, SUBLANES + bm, tn), f32), pltpu.VMEM((2, SUBLANES + bm, tn), f32),
                        pltpu.VMEM((nn, 2, CONV_W - 1, tn), f32)],
        compiler_params=_cp("arbitrary", "arbitrary"), name="ffn_up_long",
    )(*args)
    tails = jnp.concatenate([csg, csv], axis=-1).reshape(nseq, tiles_per_seq, CONV_W - 1, f2)
    return act, tails[:, last_tile]


def _rope_tables(pos):
    half = ROT_DIM // 2
    inv_freq = ROPE_THETA ** (-jnp.arange(0, ROT_DIM, 2, dtype=f32) / ROT_DIM)
    ang = pos.astype(f32)[:, None] * inv_freq[None, :]
    cos, sin = jnp.cos(ang), jnp.sin(ang)
    n = pos.shape[0]
    z_half = jnp.zeros((n, half), f32)
    z_rest = jnp.zeros((n, DA_HD - ROT_DIM), f32)
    c = jnp.concatenate([cos, cos, jnp.ones((n, DA_HD - ROT_DIM), f32)], axis=1)
    s1 = jnp.concatenate([-sin, z_half, z_rest], axis=1)
    s2 = jnp.concatenate([z_half, sin, z_rest], axis=1)
    return c, s1, s2


def _group(x, t_real, states, prm, w, page_info):
    b, t, d = x.shape
    m = b * t
    depth = prm["norm_mix_pre"].shape[0]
    h = x.reshape(m, d)
    out = dict(k=[], v=[], wkv=[], shift=[], conv=[])
    v_first = None
    xn = None
    rope = _rope_tables(states["pos"])
    for i in range(depth):
        j = i // N_MIXERS
        if i % N_MIXERS == 0:
            mixes, last = rwkv_mix(h.reshape(b, t, d), prm["norm_mix_pre"][i], prm["rw_mix"][j],
                                   states["shift"][j], t_real)
            xr, xw, xk, xv, xa, xg = (z.reshape(m, d) for z in mixes)
            r = matmul(xr, w["rw_w_r"][j])
            k = matmul(xk, w["rw_w_k"][j])
            v = matmul(xv, w["rw_w_v"][j])
            wl = matmul(matmul(xw, w["rw_w1"][j], bf16, _epi_tanh), w["rw_w2"][j])
            al = matmul(matmul(xa, w["rw_a1"][j], bf16), w["rw_a2"][j])
            g = matmul(matmul(xg, w["rw_g1"][j], bf16, _epi_sigmoid), w["rw_g2"][j])
            rows = {nm: prm["rw_" + nm][j] for nm in ("w0", "a0", "k_k", "k_a", "ln_w", "ln_b")}
            rows["r_k"] = prm["rw_r_k"][j].reshape(d)
            if j == 0:
                vres = None
                v_first = v
            else:
                vl = matmul(matmul(xv, w["rw_v1"][j - 1], bf16), w["rw_v2"][j - 1])
                vres = (vl.reshape(b, t, d), v_first.reshape(b, t, d))
                rows["v0"] = prm["rw_v0"][j - 1]
            to3 = lambda z: z.reshape(b, t, d)
            yg, s_fin = wkv7(to3(r), to3(k), to3(v), to3(wl), to3(al), to3(g), vres, rows,
                             states["wkv"][j], t_real)
            mix = matmul(yg.reshape(m, d), w["rw_w_o"][j])
            out["wkv"].append(s_fin.astype(states["wkv"][j].dtype))
            out["shift"].append(last)
        else:
            lam_init = 0.8 - 0.6 * math.exp(-0.3 * i)
            wqkv = w["da_w_qkv"][j]
            q = matmul(xn, wqkv[0], bf16, _epi_rope, rope, t)
            k = matmul(xn, wqkv[1], f32, _epi_rope, rope, t)
            v = matmul(xn, wqkv[2])
            q3, k3, v3 = q.reshape(b, t, d), k.reshape(b, t, d), v.reshape(b, t, d)
            if page_info is None:
                o = diff_attn_prompt(q3, k3, v3, prm["da_lambda"][j], prm["da_subln"][j], lam_init)
            else:
                cache_k, cache_v, page_table = page_info
                o = diff_attn_cached(q3, k3, v3, cache_k, cache_v, j, page_table, prm["da_lambda"][j],
                                     prm["da_subln"][j], lam_init, t_real)
            mix = matmul(o.reshape(m, d), w["da_w_o"][j])
            out["k"].append(k3)
            out["v"].append(v3)
        h, xn = resid_norm(h, mix, prm["norm_mix_post"][i], prm["norm_mem_pre"][i])
        mk, mv = states["mem_k"][i], states["mem_v"][i]
        xw_ = mk.shape[-1] * mk.shape[-2]
        q = matmul(xn, w["xa_w_q"][i], bf16)
        o = mem_attend(q.reshape(b, t, xw_), mk.reshape(b, -1, xw_), mv.reshape(b, -1, xw_))
        xa_out = matmul(o.reshape(m, xw_), w["xa_w_o"][i])
        h, xn = resid_norm(h, xa_out, prm["norm_mem_post"][i], prm["norm_ffn_pre"][i])
        act, conv_state = ffn_up(xn, w["ffn_w_up"][i], prm["ffn_conv_w"][i], prm["ffn_conv_b"][i],
                                 states["conv"][i], t, t_real)
        f = matmul(act, w["ffn_w_down"][i])
        out["conv"].append(conv_state)
        nxt = i + 1
        g_next = prm["norm_mix_pre"][nxt] if (nxt < depth and nxt % N_MIXERS != 0) else None
        h, xn = resid_norm(h, f, prm["norm_ffn_post"][i], g_next)
    return h.reshape(b, t, d), out


def kernel(x_prompt, x_sample, cache_k, cache_v, state_wkv, state_shift, cache_mem_k, cache_mem_v, state_ffn_conv, page_table, mem_prompt, norm_mix_pre, norm_mix_post, norm_mem_pre, norm_mem_post, norm_ffn_pre, norm_ffn_post, mem_norm, rw_mix, rw_w_r, rw_w_k, rw_w_v, rw_w_o, rw_w0, rw_w1, rw_w2, rw_a0, rw_a1, rw_a2, rw_v0, rw_v1, rw_v2, rw_g1, rw_g2, rw_k_k, rw_k_a, rw_r_k, rw_ln_w, rw_ln_b, da_w_qkv, da_w_o, da_lambda, da_subln, xa_w_q, xa_w_kv, xa_w_o, ffn_w_up, ffn_conv_w, ffn_conv_b, ffn_w_down):
    b, t, d = x_prompt.shape
    db, ds, _ = x_sample.shape
    depth = norm_mix_pre.shape[0]
    n_rwkv = state_wkv.shape[0]
    nh_rw = d // RW_HS
    f2 = ffn_w_up.shape[-1]
    past_len = page_table.shape[1] * cache_k.shape[2]

    prm = dict(norm_mix_pre=norm_mix_pre, norm_mix_post=norm_mix_post, norm_mem_pre=norm_mem_pre,
               norm_mem_post=norm_mem_post, norm_ffn_pre=norm_ffn_pre, norm_ffn_post=norm_ffn_post,
               rw_mix=rw_mix, rw_w0=rw_w0, rw_a0=rw_a0, rw_v0=rw_v0, rw_k_k=rw_k_k, rw_k_a=rw_k_a,
               rw_r_k=rw_r_k, rw_ln_w=rw_ln_w, rw_ln_b=rw_ln_b, da_lambda=da_lambda, da_subln=da_subln,
               ffn_conv_w=ffn_conv_w, ffn_conv_b=ffn_conv_b)

    ch_out = lambda wt: _to_ch(wt, nh_rw).astype(bf16)

    def lora(w_in, w_out, permute=True):
        rank = _round_up(w_in.shape[-1], LANES)
        w_out = _to_ch(w_out, nh_rw) if permute else w_out
        return _pad_to(w_in, 2, rank).astype(bf16), _pad_to(w_out, 1, rank).astype(bf16)

    w = dict(rw_w_r=ch_out(rw_w_r), rw_w_k=ch_out(rw_w_k), rw_w_v=ch_out(rw_w_v),
             rw_w_o=jnp.swapaxes(_to_ch(jnp.swapaxes(rw_w_o, 1, 2), nh_rw), 1, 2).astype(bf16),
             da_w_o=da_w_o.astype(bf16), xa_w_q=xa_w_q.astype(bf16),
             xa_w_kv=xa_w_kv.astype(bf16), xa_w_o=xa_w_o.astype(bf16), ffn_w_up=ffn_w_up.astype(bf16),
             ffn_w_down=ffn_w_down.astype(bf16))
    w["rw_w1"], w["rw_w2"] = lora(rw_w1, rw_w2)
    w["rw_a1"], w["rw_a2"] = lora(rw_a1, rw_a2)
    w["rw_v1"], w["rw_v2"] = lora(rw_v1, rw_v2)
    w["rw_g1"], w["rw_g2"] = lora(rw_g1, rw_g2)
    w["da_w_qkv"] = [[da_w_qkv[jj, :, c * d:(c + 1) * d].astype(bf16) for c in range(3)]
                     for jj in range(da_w_qkv.shape[0])]

    mem_rows = mem_prompt.reshape(-1, d)
    mem_k, mem_v = [], []
    for i in range(depth):
        kv = matmul(rms_norm_rows(mem_rows, mem_norm[i]), w["xa_w_kv"][i])
        xw_ = kv.shape[1] // 2
        hd = xw_ // XA_HEADS
        mem_k.append(kv[:, :xw_].reshape(b, -1, XA_HEADS, hd))
        mem_v.append(kv[:, xw_:].reshape(b, -1, XA_HEADS, hd))

    states_p = dict(pos=jnp.arange(t), shift=jnp.zeros((n_rwkv, b, d), f32),
                    wkv=jnp.zeros((n_rwkv, b, nh_rw, RW_HS, RW_HS), f32), mem_k=mem_k, mem_v=mem_v,
                    conv=jnp.zeros((depth, b, CONV_W - 1, f2), f32))
    hp, out_p = _group(x_prompt, t, states_p, prm, w, None)

    tpad = _round_up(ds, SUBLANES)
    states_s = dict(pos=past_len + jnp.arange(tpad), shift=state_shift, wkv=state_wkv,
                    mem_k=cache_mem_k, mem_v=cache_mem_v, conv=state_ffn_conv)
    hs, out_s = _group(_pad_to(x_sample, 1, tpad), ds, states_s, prm, w, (cache_k, cache_v, page_table))

    nh_da = d // (2 * DA_HD)
    heads = lambda z, bb, tt: z.reshape(bb, tt, nh_da, 2 * DA_HD)
    return (hp, hs[:, :ds],
            jnp.stack([heads(z, b, t) for z in out_p["k"]]), jnp.stack([heads(z, b, t) for z in out_p["v"]]),
            jnp.stack([heads(z, db, tpad)[:, :ds] for z in out_s["k"]]),
            jnp.stack([heads(z, db, tpad)[:, :ds] for z in out_s["v"]]),
            jnp.stack(out_p["wkv"]), jnp.stack(out_s["wkv"]),
            jnp.stack(out_p["shift"]), jnp.stack(out_s["shift"]),
            jnp.stack(mem_k), jnp.stack(mem_v),
            jnp.stack(out_p["conv"]), jnp.stack(out_s["conv"]))
```

```python
import functools
import math

import jax
import jax.numpy as jnp
from jax import lax
from jax.experimental import pallas as pl
from jax.experimental.pallas import tpu as pltpu

f32 = jnp.float32
bf16 = jnp.bfloat16

NORM_EPS = 1e-6
NEG_INF = -1e30
N_MIXERS = 2
RW_HS = 64
RW_GN_EPS = 64e-5
DA_HD = 128
ROT_DIM = DA_HD // 4
ROPE_THETA = 500000.0
XA_HEADS = 4
CONV_W = 3

LANES = 128
SUBLANES = 8
VMEM_LIMIT_BYTES = 56 * 1024 * 1024

TILES = dict(
    norm_bm=128,
    mm_bm=1024, mm_bn=512, mm_bk=5504,
    mix_bt=128,
    wkv_tc=16,
    attn_tq=512,
    mem_tq=512,
    ffn_bm=1024, ffn_tn=256,
    ffn_k_chunks=8,
)


def _cp(*sem):
    return pltpu.CompilerParams(dimension_semantics=sem, vmem_limit_bytes=VMEM_LIMIT_BYTES)


def _rms(x, g):
    return x * lax.rsqrt(jnp.mean(x * x, axis=-1, keepdims=True) + NORM_EPS) * g


def _pad_to(x, axis, size):
    if x.shape[axis] == size:
        return x
    pad = [(0, 0)] * x.ndim
    pad[axis] = (0, size - x.shape[axis])
    return jnp.pad(x, pad)


def _round_up(n, m):
    return (n + m - 1) // m * m


def _norm_kernel(x_ref, g_ref, o_ref):
    o_ref[...] = _rms(x_ref[...], g_ref[...]).astype(o_ref.dtype)


def rms_norm_rows(x, g, out_dtype=bf16):
    m, d = x.shape
    bm = min(TILES["norm_bm"], m)
    return pl.pallas_call(
        _norm_kernel,
        grid=(m // bm,),
        in_specs=[pl.BlockSpec((bm, d), lambda i: (i, 0)), pl.BlockSpec((1, d), lambda i: (0, 0))],
        out_specs=pl.BlockSpec((bm, d), lambda i: (i, 0)),
        out_shape=jax.ShapeDtypeStruct((m, d), out_dtype),
        compiler_params=_cp("parallel"), name="rms_norm",
    )(x, g.reshape(1, d))


def _resid_norm_kernel(x_ref, y_ref, gp_ref, gn_ref, xo_ref, xn_ref):
    xnew = x_ref[...] + _rms(y_ref[...], gp_ref[...])
    xo_ref[...] = xnew
    xn_ref[...] = _rms(xnew, gn_ref[...]).astype(xn_ref.dtype)


def _resid_kernel(x_ref, y_ref, gp_ref, xo_ref):
    xo_ref[...] = x_ref[...] + _rms(y_ref[...], gp_ref[...])


def resid_norm(x, y, g_post, g_next=None):
    m, d = x.shape
    bm = min(TILES["norm_bm"], m)
    row = pl.BlockSpec((bm, d), lambda i: (i, 0))
    gain = pl.BlockSpec((1, d), lambda i: (0, 0))
    if g_next is None:
        return pl.pallas_call(
            _resid_kernel, grid=(m // bm,), in_specs=[row, row, gain], out_specs=row,
            out_shape=jax.ShapeDtypeStruct((m, d), f32), compiler_params=_cp("parallel"), name="resid",
        )(x, y, g_post.reshape(1, d)), None
    return pl.pallas_call(
        _resid_norm_kernel, grid=(m // bm,), in_specs=[row, row, gain, gain], out_specs=[row, row],
        out_shape=[jax.ShapeDtypeStruct((m, d), f32), jax.ShapeDtypeStruct((m, d), bf16)],
        compiler_params=_cp("parallel"), name="resid_norm",
    )(x, y, g_post.reshape(1, d), g_next.reshape(1, d))


def _epi_none(acc):
    return acc


def _epi_tanh(acc):
    return jnp.tanh(acc)


def _epi_sigmoid(acc):
    return jax.nn.sigmoid(acc)


def _epi_rope(acc, c_ref, s1_ref, s2_ref):
    c, s1, s2 = c_ref[...], s1_ref[...], s2_ref[...]
    half = ROT_DIM // 2
    outs = []
    for grp in range(acc.shape[1] // DA_HD):
        x = acc[:, grp * DA_HD:(grp + 1) * DA_HD]
        outs.append(x * c + pltpu.roll(x, DA_HD - half, 1) * s1 + pltpu.roll(x, half, 1) * s2)
    return jnp.concatenate(outs, axis=1) if len(outs) > 1 else outs[0]


def _mm_kernel(*refs, nk, epilogue, n_extra):
    a_ref, b_ref = refs[0], refs[1]
    extra = refs[2:2 + n_extra]
    o_ref = refs[2 + n_extra]
    part = jnp.dot(a_ref[...], b_ref[...], preferred_element_type=f32)
    if nk == 1:
        o_ref[...] = epilogue(part, *extra).astype(o_ref.dtype)
        return
    acc_ref = refs[3 + n_extra]
    k = pl.program_id(2)

    @pl.when(k == 0)
    def _():
        acc_ref[...] = part

    @pl.when(k > 0)
    def _():
        acc_ref[...] += part

    @pl.when(k == nk - 1)
    def _():
        o_ref[...] = epilogue(acc_ref[...], *extra).astype(o_ref.dtype)


def _largest_tile(n, cap, unit):
    if n <= cap:
        return n
    best = None
    for t in range(unit, cap + 1, unit):
        if n % t == 0:
            best = t
    assert best is not None, (n, cap, unit)
    return best


def matmul(a, b, out_dtype=f32, epilogue=_epi_none, row_tables=(), table_rows=None, layer=None, cols=None):
    m, k = a.shape
    k2 = b.shape[-2]
    col0, n = cols if cols is not None else (0, b.shape[-1])
    assert k == k2 and a.dtype == bf16 and b.dtype == bf16 and (b.ndim == 3) == (layer is not None)
    bm = _largest_tile(m, TILES["mm_bm"], SUBLANES)
    bn = _largest_tile(n, TILES["mm_bn"], LANES)
    bk = _largest_tile(k, TILES["mm_bk"], LANES)
    nk = k // bk
    assert col0 % bn == 0
    jb = col0 // bn
    if layer is None:
        b_spec = pl.BlockSpec((bk, bn), lambda i, j, kk: (kk, j + jb))
    else:
        b_spec = pl.BlockSpec((None, bk, bn), lambda i, j, kk: (layer, kk, j + jb))
    in_specs = [pl.BlockSpec((bm, bk), lambda i, j, kk: (i, kk)), b_spec]
    tables = []
    for t in row_tables:
        if table_rows < bm:
            assert bm % table_rows == 0
            t = jnp.tile(t, (bm // table_rows, 1))
        assert t.shape[0] % bm == 0
        nrep = t.shape[0] // bm
        tables.append(t)
        in_specs.append(pl.BlockSpec((bm, t.shape[1]), lambda i, j, kk, nrep=nrep: (i % nrep, 0)))
    scratch = [pltpu.VMEM((bm, bn), f32)] if nk > 1 else []
    return pl.pallas_call(
        functools.partial(_mm_kernel, nk=nk, epilogue=epilogue, n_extra=len(tables)),
        grid=(m // bm, n // bn, nk),
        in_specs=in_specs,
        out_specs=pl.BlockSpec((bm, bn), lambda i, j, kk: (i, j)),
        out_shape=jax.ShapeDtypeStruct((m, n), out_dtype),
        scratch_shapes=scratch,
        compiler_params=_cp("parallel", "parallel", "arbitrary"), name="matmul_" + epilogue.__name__[5:],
    )(a, b, *tables)


def _mix_kernel(h_ref, g_ref, mix_ref, prev_ref, *refs, bt, last_tile, last_row):
    outs, last_ref, carry_ref = refs[:6], refs[6], refs[7]
    tj = pl.program_id(1)
    xn = _rms(h_ref[0], g_ref[...])

    @pl.when(tj == 0)
    def _():
        carry_ref[SUBLANES - 1:SUBLANES, :] = prev_ref[0]

    prev_row = carry_ref[SUBLANES - 1:SUBLANES, :]
    row = lax.broadcasted_iota(jnp.int32, xn.shape, 0)
    shifted = jnp.where(row == 0, prev_row, pltpu.roll(xn, 1, 0))
    xx = shifted - xn
    for m_i in range(6):
        outs[m_i][0] = (xn + xx * mix_ref[m_i:m_i + 1, :]).astype(outs[m_i].dtype)
    carry_ref[...] = xn[bt - SUBLANES:bt, :]

    @pl.when(tj == last_tile)
    def _():
        last_ref[0] = xn[last_row:last_row + 1, :]


def rwkv_mix(h, g, mix, x_prev, t_real):
    b, t, d = h.shape
    bt = min(TILES["mix_bt"], t)
    seq = pl.BlockSpec((1, bt, d), lambda bi, tj: (bi, tj, 0))
    one = pl.BlockSpec((1, 1, d), lambda bi, tj: (bi, 0, 0))
    outs = pl.pallas_call(
        functools.partial(_mix_kernel, bt=bt, last_tile=(t_real - 1) // bt, last_row=(t_real - 1) % bt),
        grid=(b, t // bt),
        in_specs=[seq, pl.BlockSpec((1, d), lambda bi, tj: (0, 0)),
                  pl.BlockSpec((6, d), lambda bi, tj: (0, 0)), one],
        out_specs=[seq] * 6 + [one],
        out_shape=[jax.ShapeDtypeStruct((b, t, d), bf16)] * 6 + [jax.ShapeDtypeStruct((b, 1, d), f32)],
        scratch_shapes=[pltpu.VMEM((SUBLANES, d), f32)],
        compiler_params=_cp("parallel", "arbitrary"), name="rwkv_mix",
    )(h, g.reshape(1, d), mix, x_prev.reshape(b, 1, d))
    return outs[:6], outs[6].reshape(b, d)


def _wkv_kernel(*refs, has_vres, tc, t_real, n_chunks):
    it = iter(refs)
    r_ref, k_ref, v_ref, wl_ref, al_ref, g_ref = (next(it) for _ in range(6))
    vl_ref, vf_ref = (next(it), next(it)) if has_vres else (None, None)
    w0_ref, a0_ref = next(it), next(it)
    v0_ref = next(it) if has_vres else None
    kk_ref, ka_ref, rk_ref, lnw_ref, lnb_ref, s0_ref = (next(it) for _ in range(6))
    yg_ref, sout_ref = next(it), next(it)
    s_scr = next(it)
    r_t, w_t, k_t, v_t, a_t, b_t, y_t = (next(it) for _ in range(7))
    ops_ct = [next(it) for _ in range(6)]
    v_ct = ops_ct[3]

    nvb = RW_HS // SUBLANES
    n_cp = RW_HS // 2
    ci = pl.program_id(1)

    @pl.when(ci == 0)
    def _():
        s_scr[...] = s0_ref[0]

    y_t[...] = jnp.zeros(y_t.shape, f32)

    low = lax.broadcasted_iota(jnp.int32, (tc, LANES), 1) < RW_HS

    def gather(ref, cp):
        cols = slice(cp * LANES, (cp + 1) * LANES)
        g0, g1 = ref[0, :, cols], ref[1, :, cols]
        return (jnp.where(low, g0, pltpu.roll(g1, RW_HS, 1)), jnp.where(low, pltpu.roll(g0, RW_HS, 1), g1))

    def scatter(x_even, x_odd):
        return (jnp.where(low, x_even, pltpu.roll(x_odd, RW_HS, 1)), jnp.where(low, pltpu.roll(x_even, RW_HS, 1), x_odd))

    def prow(ref, c):
        return ref[c:c + 1, :]

    ss = jnp.zeros((tc, LANES), f32)
    for cp in range(n_cp):
        for par, kc in enumerate(gather(k_ref, cp)):
            kk = kc * prow(kk_ref, 2 * cp + par)
            ss = ss + kk * kk
    kk_den = jnp.maximum(jnp.sqrt(ss), 1e-12)

    coef = jnp.zeros((tc, LANES), f32)
    for cp in range(n_cp):
        rr, kr, vr, wlr, alr = (gather(ref, cp) for ref in (r_ref, k_ref, v_ref, wl_ref, al_ref))
        if has_vres:
            vlr, vfr = gather(vl_ref, cp), gather(vf_ref, cp)
        for par in range(2):
            c = 2 * cp + par
            r, k, v = rr[par], kr[par], vr[par]
            z = -(prow(w0_ref, c) + wlr[par])
            softplus = jnp.maximum(z, 0.0) + jnp.log1p(jnp.exp(-jnp.abs(z)))
            w = jnp.exp(-jnp.exp(-softplus - 0.5))
            a = jax.nn.sigmoid(prow(a0_ref, c) + alr[par])
            if has_vres:
                v = v + (vfr[par] - v) * jax.nn.sigmoid(prow(v0_ref, c) + vlr[par])
            kk = k * prow(kk_ref, c) / kk_den
            k2 = k * (1.0 + (a - 1.0) * prow(ka_ref, c))
            coef = coef + r * k2 * prow(rk_ref, c)
            for dst, val in zip(ops_ct, (r, w, k2, v, -kk, kk * a)):
                dst[c * tc:(c + 1) * tc, :] = val

    for src, dst in zip(ops_ct, (r_t, w_t, k_t, v_t, a_t, b_t)):
        for tt in range(tc):
            dst[tt * RW_HS:(tt + 1) * RW_HS, :] = src[pl.ds(tt, RW_HS, stride=tc), :]

    def bcast_row(tile, i):
        return jnp.broadcast_to(tile[i:i + 1, :], (SUBLANES, LANES))

    def step(t, carry):
        base = pl.multiple_of(t * RW_HS, RW_HS)
        acc = [[jnp.zeros((SUBLANES, LANES), f32) for _ in range(2)] for _ in range(nvb)]
        for kb in range(nvb):
            a_tile = a_t[pl.ds(base + kb * SUBLANES, SUBLANES), :]
            for i in range(SUBLANES):
                k_i = kb * SUBLANES + i
                a_k = bcast_row(a_tile, i)
                for vb in range(nvb):
                    acc[vb][i % 2] = acc[vb][i % 2] + s_scr[vb, k_i] * a_k
        sa = [acc[vb][0] + acc[vb][1] for vb in range(nvb)]
        vt = [v_t[pl.ds(base + vb * SUBLANES, SUBLANES), :] for vb in range(nvb)]
        yacc = [[jnp.zeros((SUBLANES, LANES), f32) for _ in range(2)] for _ in range(nvb)]
        for kb in range(nvb):
            rows = pl.ds(base + kb * SUBLANES, SUBLANES)
            w_tile, b_tile, k_tile, r_tile = w_t[rows, :], b_t[rows, :], k_t[rows, :], r_t[rows, :]
            for i in range(SUBLANES):
                k_i = kb * SUBLANES + i
                w_k, b_k, k_k, r_k = (bcast_row(tl, i) for tl in (w_tile, b_tile, k_tile, r_tile))
                for vb in range(nvb):
                    s = s_scr[vb, k_i] * w_k + sa[vb] * b_k + vt[vb] * k_k
                    s_scr[vb, k_i] = s
                    yacc[vb][i % 2] = yacc[vb][i % 2] + s * r_k
        for vb in range(nvb):
            y_t[pl.ds(base + vb * SUBLANES, SUBLANES), :] = yacc[vb][0] + yacc[vb][1]
        return carry

    if t_real % tc == 0:
        lax.fori_loop(0, tc, step, 0)
    else:
        lax.fori_loop(0, jnp.clip(t_real - ci * tc, 0, tc), step, 0)

    def y_tile(c):
        return y_t[pl.ds(c, tc, stride=RW_HS), :]

    mu = jnp.zeros((tc, LANES), f32)
    for c in range(RW_HS):
        mu = mu + y_tile(c)
    mu = mu * (1.0 / RW_HS)
    var = jnp.zeros((tc, LANES), f32)
    for c in range(RW_HS):
        d = y_tile(c) - mu
        var = var + d * d
    rstd = lax.rsqrt(var * (1.0 / RW_HS) + RW_GN_EPS)
    for cp in range(n_cp):
        gates = gather(g_ref, cp)
        outs = []
        for par in range(2):
            c = 2 * cp + par
            yn = (y_tile(c) - mu) * rstd * prow(lnw_ref, c) + prow(lnb_ref, c)
            outs.append((yn + coef * v_ct[c * tc:(c + 1) * tc, :]) * gates[par])
        o0, o1 = scatter(outs[0], outs[1])
        cols = slice(cp * LANES, (cp + 1) * LANES)
        yg_ref[0, :, cols] = o0.astype(yg_ref.dtype)
        yg_ref[1, :, cols] = o1.astype(yg_ref.dtype)

    @pl.when(ci == n_chunks - 1)
    def _():
        sout_ref[0] = s_scr[...]


def _to_ch(x, nh):
    lead = x.shape[:-1]
    return jnp.swapaxes(x.reshape(*lead, nh, RW_HS), -1, -2).reshape(*lead, nh * RW_HS)


def _state_to_tiles(s):
    b, nh = s.shape[:2]
    s = s.reshape(b // 2, 2, nh, RW_HS // SUBLANES, SUBLANES, RW_HS)
    return s.transpose(0, 3, 5, 4, 1, 2).reshape(b // 2, RW_HS // SUBLANES, RW_HS, SUBLANES, 2 * nh)


def _state_from_tiles(st, nh):
    nb = st.shape[0]
    s = st.reshape(nb, RW_HS // SUBLANES, RW_HS, SUBLANES, 2, nh).transpose(0, 4, 5, 1, 3, 2)
    return s.reshape(nb * 2, nh, RW_HS, RW_HS)


def wkv7(r, k, v, wl, al, g, vres, rows, s0, t_real):
    b, t, d = r.shape
    nh = d // RW_HS
    assert 2 * nh == LANES and b % 2 == 0
    tc = min(TILES["wkv_tc"], t)
    assert tc % SUBLANES == 0 and t % tc == 0
    n_chunks = t // tc
    assert t_real % tc == 0 or n_chunks == 1
    has_vres = vres is not None
    nvb = RW_HS // SUBLANES
    seq = pl.BlockSpec((2, tc, d), lambda bi, ci: (bi, ci, 0))
    prow = pl.BlockSpec((RW_HS, LANES), lambda bi, ci: (0, 0))
    st = pl.BlockSpec((1, nvb, RW_HS, SUBLANES, LANES), lambda bi, ci: (bi, 0, 0, 0, 0))
    seq_in = [r, k, v, wl, al, g] + (list(vres) if has_vres else [])
    names = ["w0", "a0"] + (["v0"] if has_vres else []) + ["k_k", "k_a", "r_k", "ln_w", "ln_b"]
    row_in = [jnp.tile(rows[nm].reshape(nh, RW_HS).T, (1, 2)) for nm in names]
    op_scr = pltpu.VMEM((tc * RW_HS, LANES), f32)
    yg, s_out = pl.pallas_call(
        functools.partial(_wkv_kernel, has_vres=has_vres, tc=tc, t_real=t_real, n_chunks=n_chunks),
        grid=(b // 2, n_chunks),
        in_specs=[seq] * len(seq_in) + [prow] * len(row_in) + [st],
        out_specs=[seq, st],
        out_shape=[jax.ShapeDtypeStruct((b, t, d), bf16),
                   jax.ShapeDtypeStruct((b // 2, nvb, RW_HS, SUBLANES, LANES), f32)],
        scratch_shapes=[pltpu.VMEM((nvb, RW_HS, SUBLANES, LANES), f32)] + [op_scr] * 13,
        compiler_params=_cp("parallel", "arbitrary"), name="wkv7",
    )(*seq_in, *row_in, _state_to_tiles(s0.astype(f32)))
    return yg, _state_from_tiles(s_out, nh)


def _diff_lambda(lam_ref, lam_init):
    lp = lam_ref[...]
    s1 = jnp.sum(lp[0:1] * lp[1:2], axis=-1, keepdims=True)
    s2 = jnp.sum(lp[2:3] * lp[3:4], axis=-1, keepdims=True)
    return jnp.exp(s1) - jnp.exp(s2) + lam_init


def _online_softmax_step(s, v_bf, m_ref, l_ref, acc_ref, c):
    m_prev = m_ref[c]
    m_new = jnp.maximum(m_prev, jnp.max(s, axis=-1, keepdims=True))
    alpha = jnp.exp(m_prev - m_new)
    p = jnp.exp(s - m_new)
    l_ref[c] = alpha * l_ref[c] + jnp.sum(p, axis=-1, keepdims=True)
    acc_ref[c] = alpha * acc_ref[c] + jnp.dot(p.astype(bf16), v_bf, preferred_element_type=f32)
    m_ref[c] = m_new


def _diff_prompt_kernel(q_ref, k_ref, v_ref, lam_ref, sub_ref, o_ref, m_ref, l_ref, acc_ref,
                        *, tq, nk, lam_init, scale):
    i, j = pl.program_id(2), pl.program_id(3)

    @pl.when(j == 0)
    def _():
        m_ref[...] = jnp.full(m_ref.shape, NEG_INF, f32)
        l_ref[...] = jnp.zeros(l_ref.shape, f32)
        acc_ref[...] = jnp.zeros(acc_ref.shape, f32)

    def block(diagonal):
        v_bf = v_ref[0].astype(bf16)
        for c in range(2):
            qc = q_ref[0, :, c * DA_HD:(c + 1) * DA_HD]
            kc = k_ref[0, :, c * DA_HD:(c + 1) * DA_HD].astype(bf16)
            s = lax.dot_general(qc, kc, (((1,), (1,)), ((), ())), preferred_element_type=f32) * scale
            if diagonal:
                q_pos = lax.broadcasted_iota(jnp.int32, (tq, tq), 0)
                k_pos = lax.broadcasted_iota(jnp.int32, (tq, tq), 1)
                s = jnp.where(k_pos <= q_pos, s, NEG_INF)
            _online_softmax_step(s, v_bf, m_ref, l_ref, acc_ref, c)

    @pl.when(j < i)
    def _():
        block(False)

    @pl.when(j == i)
    def _():
        block(True)

    @pl.when(j == nk - 1)
    def _():
        lam = _diff_lambda(lam_ref, lam_init)
        o = acc_ref[0] / l_ref[0] - lam * (acc_ref[1] / l_ref[1])
        o_ref[0] = (_rms(o, sub_ref[...]) * (1.0 - lam_init)).astype(o_ref.dtype)


def diff_attn_prompt(q, k, v, lam_params, subln, lam_init):
    b, t, d = q.shape
    hw = 2 * DA_HD
    nh = d // hw
    tq = min(TILES["attn_tq"], t)
    nq = t // tq
    qspec = pl.BlockSpec((1, tq, hw), lambda bi, hi, i, j: (bi, i, hi))
    kspec = pl.BlockSpec((1, tq, hw), lambda bi, hi, i, j: (bi, jnp.minimum(j, i), hi))
    return pl.pallas_call(
        functools.partial(_diff_prompt_kernel, tq=tq, nk=nq, lam_init=lam_init, scale=DA_HD ** -0.5),
        grid=(b, nh, nq, nq),
        in_specs=[qspec, kspec, kspec,
                  pl.BlockSpec((4, DA_HD), lambda bi, hi, i, j: (0, 0)),
                  pl.BlockSpec((1, hw), lambda bi, hi, i, j: (0, 0))],
        out_specs=qspec,
        out_shape=jax.ShapeDtypeStruct((b, t, d), bf16),
        scratch_shapes=[pltpu.VMEM((2, tq, 1), f32), pltpu.VMEM((2, tq, 1), f32), pltpu.VMEM((2, tq, hw), f32)],
        compiler_params=_cp("parallel", "parallel", "parallel", "arbitrary"), name="diff_attn_prompt",
    )(q, k, v, lam_params, subln.reshape(1, hw))


def _diff_cached_kernel(pt_ref, q_ref, *refs, n_steps, pps, nh, tpad, t_real, lam_init, scale):
    kp_refs, vp_refs = refs[:pps], refs[pps:2 * pps]
    kn_ref, vn_ref, lam_ref, sub_ref, o_ref, m_ref, l_ref, acc_ref = refs[2 * pps:]
    p = pl.program_id(1)
    nq = nh * tpad

    @pl.when(p == 0)
    def _():
        m_ref[...] = jnp.full(m_ref.shape, NEG_INF, f32)
        l_ref[...] = jnp.zeros(l_ref.shape, f32)
        acc_ref[...] = jnp.zeros(acc_ref.shape, f32)

    def update(k3, v3, causal):
        nkeys = k3.shape[0]
        k2 = k3.reshape(nkeys * nh, 2 * DA_HD).astype(bf16)
        v2 = v3.reshape(nkeys * nh, 2 * DA_HD).astype(bf16)
        row = lax.broadcasted_iota(jnp.int32, (nq, nkeys * nh), 0)
        col = lax.broadcasted_iota(jnp.int32, (nq, nkeys * nh), 1)
        keep = (row // tpad) == (col % nh)
        if causal:
            key = col // nh
            keep = keep & (key <= row % tpad) & (key < t_real)
        ps = []
        for c in range(2):
            s = lax.dot_general(q_ref[0, c], k2[:, c * DA_HD:(c + 1) * DA_HD], (((1,), (1,)), ((), ())),
                                preferred_element_type=f32) * scale
            s = jnp.where(keep, s, NEG_INF)
            m_prev = m_ref[c]
            m_new = jnp.maximum(m_prev, jnp.max(s, axis=-1, keepdims=True))
            alpha = jnp.exp(m_prev - m_new)
            pr = jnp.exp(s - m_new)
            l_ref[c] = alpha * l_ref[c] + jnp.sum(pr, axis=-1, keepdims=True)
            acc_ref[c] = alpha * acc_ref[c]
            m_ref[c] = m_new
            ps.append(pr.astype(bf16))
        pv = jnp.dot(jnp.concatenate(ps, axis=0), v2, preferred_element_type=f32)
        acc_ref[0] += pv[:nq]
        acc_ref[1] += pv[nq:]

    @pl.when(p < n_steps)
    def _():
        for kp_ref, vp_ref in zip(kp_refs, vp_refs):
            update(kp_ref[...], vp_ref[...], False)

    @pl.when(p == n_steps)
    def _():
        update(kn_ref[0], vn_ref[0], True)
        lam = _diff_lambda(lam_ref, lam_init)
        o = acc_ref[0] / l_ref[0] - lam * (acc_ref[1] / l_ref[1])
        o_ref[0] = (_rms(o, sub_ref[...]) * (1.0 - lam_init)).astype(o_ref.dtype)


def diff_attn_cached(q, k_new, v_new, cache_k, cache_v, layer, page_table, lam_params, subln, lam_init, t_real):
    b, tpad, d = q.shape
    hw = 2 * DA_HD
    nh = d // hw
    n_pages = page_table.shape[1]
    page = cache_k.shape[2]
    nq = nh * tpad
    qm = q.reshape(b, tpad, nh, 2, DA_HD).transpose(0, 3, 2, 1, 4).reshape(b, 2, nq, DA_HD)
    kn = k_new.reshape(b, tpad, nh, hw)
    vn = v_new.reshape(b, tpad, nh, hw)
    pps = 2 if n_pages % 2 == 0 else 1
    n_steps = n_pages // pps
    last = n_steps - 1

    def pspec(which):
        return pl.BlockSpec((None, None, page, nh, hw),
                            lambda bi, p, pt: (layer, pt[bi, jnp.minimum(p, last) * pps + which], 0, 0, 0))

    pspecs = [pspec(which) for which in range(pps)]
    nspec = pl.BlockSpec((1, tpad, nh, hw), lambda bi, p, pt: (bi, 0, 0, 0))
    grid_spec = pltpu.PrefetchScalarGridSpec(
        num_scalar_prefetch=1,
        grid=(b, n_steps + 1),
        in_specs=[pl.BlockSpec((1, 2, nq, DA_HD), lambda bi, p, pt: (bi, 0, 0, 0))] + pspecs + pspecs
        + [nspec, nspec,
           pl.BlockSpec((4, DA_HD), lambda bi, p, pt: (0, 0)),
           pl.BlockSpec((1, hw), lambda bi, p, pt: (0, 0))],
        out_specs=pl.BlockSpec((1, nq, hw), lambda bi, p, pt: (bi, 0, 0)),
        scratch_shapes=[pltpu.VMEM((2, nq, 1), f32), pltpu.VMEM((2, nq, 1), f32), pltpu.VMEM((2, nq, hw), f32)],
    )
    o = pl.pallas_call(
        functools.partial(_diff_cached_kernel, n_steps=n_steps, pps=pps, nh=nh, tpad=tpad, t_real=t_real,
                          lam_init=lam_init, scale=DA_HD ** -0.5),
        grid_spec=grid_spec,
        out_shape=jax.ShapeDtypeStruct((b, nq, hw), bf16),
        compiler_params=_cp("parallel", "arbitrary"), name="diff_attn_cached",
    )(page_table, qm, *([cache_k] * pps), *([cache_v] * pps), kn, vn, lam_params, subln.reshape(1, hw))
    return o.reshape(b, nh, tpad, hw).transpose(0, 2, 1, 3).reshape(b, tpad, d)


def _mem_attn_kernel(q_ref, k_ref, v_ref, o_ref, *, hd, scale):
    for h in range(XA_HEADS):
        cols = slice(h * hd, (h + 1) * hd)
        s = lax.dot_general(q_ref[0, :, cols], k_ref[0, :, cols].astype(bf16), (((1,), (1,)), ((), ())),
                            preferred_element_type=f32) * scale
        e = jnp.exp(s - jnp.max(s, axis=-1, keepdims=True))
        p = e / jnp.sum(e, axis=-1, keepdims=True)
        o_ref[0, :, cols] = jnp.dot(p.astype(bf16), v_ref[0, :, cols].astype(bf16),
                                    preferred_element_type=f32).astype(o_ref.dtype)


def mem_attend(q, mk, mv):
    b, t, w = q.shape
    nm = mk.shape[1]
    tq = min(TILES["mem_tq"], t)
    hd = w // XA_HEADS
    qspec = pl.BlockSpec((1, tq, w), lambda bi, i: (bi, i, 0))
    mspec = pl.BlockSpec((1, nm, w), lambda bi, i: (bi, 0, 0))
    return pl.pallas_call(
        functools.partial(_mem_attn_kernel, hd=hd, scale=hd ** -0.5),
        grid=(b, t // tq),
        in_specs=[qspec, mspec, mspec],
        out_specs=qspec,
        out_shape=jax.ShapeDtypeStruct((b, t, w), bf16),
        compiler_params=_cp("parallel", "parallel"), name="mem_attend",
    )(q, mk, mv)


def _silu_gate(gate, val):
    return gate * jax.nn.sigmoid(gate) * val


def _ffn_up_short_kernel(a_ref, wg_ref, wv_ref, cwg_ref, cwv_ref, cbg_ref, cbv_ref, pg_ref, pv_ref,
                         act_ref, csg_ref, csv_ref, *, seq_len, n_sub, last_row):
    a = a_ref[...]
    row = lax.broadcasted_iota(jnp.int32, (seq_len, act_ref.shape[1]), 0)

    def half(w_ref, cw_ref, cb_ref, p_ref, cs_ref):
        h_all = jnp.dot(a, w_ref[...].astype(bf16), preferred_element_type=f32)
        cw = cw_ref[...]
        outs = []
        for s in range(n_sub):
            h = h_all[s * seq_len:(s + 1) * seq_len]
            p0, p1 = p_ref[s, 0:1, :], p_ref[s, 1:2, :]
            r1 = jnp.where(row == 0, p1, pltpu.roll(h, 1, 0))
            r2 = jnp.where(row == 0, p0, jnp.where(row == 1, p1, pltpu.roll(h, 2, 0)))
            outs.append(cb_ref[...] + r2 * cw[0:1] + r1 * cw[1:2] + h * cw[2:3])
            cs_ref[s] = h[last_row - 1:last_row + 1]
        return jnp.concatenate(outs, axis=0) if n_sub > 1 else outs[0]

    gate = half(wg_ref, cwg_ref, cbg_ref, pg_ref, csg_ref)
    val = half(wv_ref, cwv_ref, cbv_ref, pv_ref, csv_ref)
    act_ref[...] = _silu_gate(gate, val).astype(act_ref.dtype)


def _ffn_up_long_kernel(a_ref, wg_ref, wv_ref, cwg_ref, cwv_ref, cbg_ref, cbv_ref, pg_ref, pv_ref,
                        act_ref, csg_ref, csv_ref, hbuf0, hbuf1, carry_ref,
                        *, bm, tiles_per_seq, last_row, nn, k_chunks):
    halo = SUBLANES
    i, j = pl.program_id(0), pl.program_id(1)

    @pl.when((i == 0) & (j == 0))
    def _():
        hbuf1[...] = jnp.zeros(hbuf1.shape, f32)

    def work(fill, drain):
        jt = jnp.maximum(j - 1, 0)
        first = (i % tiles_per_seq) == 0
        halves = ((cwg_ref, cbg_ref, pg_ref, csg_ref), (cwv_ref, cbv_ref, pv_ref, csv_ref))
        for slot, (_, _, p_ref, _) in enumerate(halves):
            @pl.when(first)
            def _():
                drain[slot, halo - 2:halo, :] = p_ref[0]

            @pl.when(jnp.logical_not(first))
            def _():
                drain[slot, halo - 2:halo, :] = carry_ref[jt, slot]

        rc = min(bm, 64)

        def gate_rows(r0):
            cvals = []
            for slot, (cw_ref, cb_ref, _, _) in enumerate(halves):
                cw = cw_ref[...]
                cvals.append(cb_ref[...] + drain[slot, halo - 2 + r0:halo - 2 + r0 + rc, :] * cw[0:1]
                             + drain[slot, halo - 1 + r0:halo - 1 + r0 + rc, :] * cw[1:2]
                             + drain[slot, halo + r0:halo + r0 + rc, :] * cw[2:3])
            act_ref[r0:r0 + rc, :] = _silu_gate(cvals[0], cvals[1]).astype(act_ref.dtype)

        segs = [(slot, kq) for kq in range(k_chunks) for slot in range(2)]
        row_chunks = list(range(0, bm, rc))
        per_seg = -(-len(row_chunks) // len(segs))
        kc = a_ref.shape[1] // k_chunks
        for si, (slot, kq) in enumerate(segs):
            w_ref = (wg_ref, wv_ref)[slot]
            part = jnp.dot(a_ref[:, kq * kc:(kq + 1) * kc], w_ref[kq * kc:(kq + 1) * kc, :].astype(bf16),
                           preferred_element_type=f32)
            if kq == 0:
                fill[slot, halo:halo + bm, :] = part
            else:
                fill[slot, halo:halo + bm, :] += part
            for r0 in row_chunks[si * per_seg:(si + 1) * per_seg]:
                gate_rows(r0)
        for slot, (_, _, _, cs_ref) in enumerate(halves):
            cs_ref[0] = drain[slot, halo + last_row - 1:halo + last_row + 1, :]

        @pl.when(j > 0)
        def _():
            for slot in range(2):
                carry_ref[jt, slot] = drain[slot, halo + bm - 2:halo + bm, :]

    @pl.when(j % 2 == 0)
    def _():
        work(hbuf0, hbuf1)

    @pl.when(j % 2 == 1)
    def _():
        work(hbuf1, hbuf0)


def ffn_up(xn, w_up, layer, conv_w, conv_b, prev_rows, t_seq, t_real):
    m, d = xn.shape
    f2 = w_up.shape[2]
    dff = f2 // 2
    nseq = m // t_seq
    bm = _largest_tile(m, TILES["ffn_bm"], SUBLANES)
    tn = _largest_tile(dff, TILES["ffn_tn"], LANES)
    nn = dff // tn
    cw3 = conv_w.reshape(CONV_W, f2)
    cb2 = conv_b.reshape(1, f2)
    args = (xn, w_up, w_up, cw3, cw3, cb2, cb2, prev_rows, prev_rows)
    if bm >= t_seq:
        assert bm % t_seq == 0
        n_sub = bm // t_seq
        last_row = t_real - 1
        assert last_row >= CONV_W - 2
        tail_spec = pl.BlockSpec((n_sub, CONV_W - 1, tn), lambda i, j: (i, 0, j))
        tail_shape = jax.ShapeDtypeStruct((nseq, CONV_W - 1, dff), f32)
        pspec = lambda off: pl.BlockSpec((n_sub, CONV_W - 1, tn), lambda i, j: (i, 0, j + off))
        act, csg, csv = pl.pallas_call(
            functools.partial(_ffn_up_short_kernel, seq_len=t_seq, n_sub=n_sub, last_row=last_row),
            grid=(m // bm, nn),
            in_specs=[pl.BlockSpec((bm, d), lambda i, j: (i, 0)),
                      pl.BlockSpec((None, d, tn), lambda i, j: (layer, 0, j)),
                      pl.BlockSpec((None, d, tn), lambda i, j: (layer, 0, j + nn)),
                      pl.BlockSpec((CONV_W, tn), lambda i, j: (0, j)),
                      pl.BlockSpec((CONV_W, tn), lambda i, j: (0, j + nn)),
                      pl.BlockSpec((1, tn), lambda i, j: (0, j)),
                      pl.BlockSpec((1, tn), lambda i, j: (0, j + nn)),
                      pspec(0), pspec(nn)],
            out_specs=[pl.BlockSpec((bm, tn), lambda i, j: (i, j)), tail_spec, tail_spec],
            out_shape=[jax.ShapeDtypeStruct((m, dff), bf16), tail_shape, tail_shape],
            compiler_params=_cp("parallel", "parallel"), name="ffn_up_short",
        )(*args)
        return act, jnp.concatenate([csg, csv], axis=-1)

    assert t_seq % bm == 0
    tiles_per_seq = t_seq // bm
    last_tile, last_row = (t_real - 1) // bm, (t_real - 1) % bm
    assert last_tile == tiles_per_seq - 1 and last_row >= CONV_W - 2
    jm = lambda j: jnp.minimum(j, nn - 1)
    je = lambda j: jnp.maximum(j - 1, 0)
    seq_of = lambda i: i // tiles_per_seq
    tail_spec = pl.BlockSpec((1, CONV_W - 1, tn), lambda i, j: (i, 0, je(j)))
    tail_shape = jax.ShapeDtypeStruct((nseq * tiles_per_seq, CONV_W - 1, dff), f32)
    pspec = lambda off: pl.BlockSpec((1, CONV_W - 1, tn), lambda i, j: (seq_of(i), 0, je(j) + off))
    act, csg, csv = pl.pallas_call(
        functools.partial(_ffn_up_long_kernel, bm=bm, tiles_per_seq=tiles_per_seq, last_row=last_row, nn=nn,
                          k_chunks=TILES["ffn_k_chunks"] if d % (TILES["ffn_k_chunks"] * 2 * LANES) == 0 else 1),
        grid=(m // bm, nn + 1),
        in_specs=[pl.BlockSpec((bm, d), lambda i, j: (i, 0)),
                  pl.BlockSpec((None, d, tn), lambda i, j: (layer, 0, jm(j))),
                  pl.BlockSpec((None, d, tn), lambda i, j: (layer, 0, jm(j) + nn)),
                  pl.BlockSpec((CONV_W, tn), lambda i, j: (0, je(j))),
                  pl.BlockSpec((CONV_W, tn), lambda i, j: (0, je(j) + nn)),
                  pl.BlockSpec((1, tn), lambda i, j: (0, je(j))),
                  pl.BlockSpec((1, tn), lambda i, j: (0, je(j) + nn)),
                  pspec(0), pspec(nn)],
        out_specs=[pl.BlockSpec((bm, tn), lambda i, j: (i, je(j))), tail_spec, tail_spec],
        out_shape=[jax.ShapeDtypeStruct((m, dff), bf16), tail_shape, tail_shape],
        scratch_shapes=[pltpu.VMEM((2, SUBLANES + bm, tn), f32), pltpu.VMEM((2, SUBLANES + bm, tn), f32),
                        pltpu.VMEM((nn, 2, CONV_W - 1, tn), f32)],
        compiler_params=_cp("arbitrary", "arbitrary"), name="ffn_up_long",
    )(*args)
    tails = jnp.concatenate([csg, csv], axis=-1).reshape(nseq, tiles_per_seq, CONV_W - 1, f2)
    return act, tails[:, last_tile]


def _rope_tables(pos):
    half = ROT_DIM // 2
    inv_freq = ROPE_THETA ** (-jnp.arange(0, ROT_DIM, 2, dtype=f32) / ROT_DIM)
    ang = pos.astype(f32)[:, None] * inv_freq[None, :]
    cos, sin = jnp.cos(ang), jnp.sin(ang)
    n = pos.shape[0]
    z_half = jnp.zeros((n, half), f32)
    z_rest = jnp.zeros((n, DA_HD - ROT_DIM), f32)
    c = jnp.concatenate([cos, cos, jnp.ones((n, DA_HD - ROT_DIM), f32)], axis=1)
    s1 = jnp.concatenate([-sin, z_half, z_rest], axis=1)
    s2 = jnp.concatenate([z_half, sin, z_rest], axis=1)
    return c, s1, s2


def _group(x, t_real, states, prm, w, page_info):
    b, t, d = x.shape
    m = b * t
    depth = prm["norm_mix_pre"].shape[0]
    h = x.reshape(m, d)
    out = dict(k=[], v=[], wkv=[], shift=[], conv=[])
    v_first = None
    xn = None
    rope = _rope_tables(states["pos"])
    for i in range(depth):
        j = i // N_MIXERS
        if i % N_MIXERS == 0:
            mixes, last = rwkv_mix(h.reshape(b, t, d), prm["norm_mix_pre"][i], prm["rw_mix"][j],
                                   states["shift"][j], t_real)
            xr, xw, xk, xv, xa, xg = (z.reshape(m, d) for z in mixes)
            r = matmul(xr, w["rw_w_r"], layer=j)
            k = matmul(xk, w["rw_w_k"], layer=j)
            v = matmul(xv, w["rw_w_v"], layer=j)
            wl = matmul(matmul(xw, w["rw_w1"], bf16, _epi_tanh, layer=j), w["rw_w2"], layer=j)
            al = matmul(matmul(xa, w["rw_a1"], bf16, layer=j), w["rw_a2"], layer=j)
            g = matmul(matmul(xg, w["rw_g1"], bf16, _epi_sigmoid, layer=j), w["rw_g2"], layer=j)
            rows = {nm: prm["rw_" + nm][j] for nm in ("w0", "a0", "k_k", "k_a", "ln_w", "ln_b")}
            rows["r_k"] = prm["rw_r_k"][j].reshape(d)
            if j == 0:
                vres = None
                v_first = v
            else:
                vl = matmul(matmul(xv, w["rw_v1"], bf16, layer=j - 1), w["rw_v2"], layer=j - 1)
                vres = (vl.reshape(b, t, d), v_first.reshape(b, t, d))
                rows["v0"] = prm["rw_v0"][j - 1]
            to3 = lambda z: z.reshape(b, t, d)
            yg, s_fin = wkv7(to3(r), to3(k), to3(v), to3(wl), to3(al), to3(g), vres, rows,
                             states["wkv"][j], t_real)
            mix = matmul(yg.reshape(m, d), w["rw_w_o"], layer=j)
            out["wkv"].append(s_fin.astype(states["wkv"][j].dtype))
            out["shift"].append(last)
        else:
            lam_init = 0.8 - 0.6 * math.exp(-0.3 * i)
            wqkv = w["da_w_qkv"]
            q = matmul(xn, wqkv, bf16, _epi_rope, rope, t, layer=j, cols=(0, d))
            k = matmul(xn, wqkv, f32, _epi_rope, rope, t, layer=j, cols=(d, d))
            v = matmul(xn, wqkv, layer=j, cols=(2 * d, d))
            q3, k3, v3 = q.reshape(b, t, d), k.reshape(b, t, d), v.reshape(b, t, d)
            if page_info is None:
                o = diff_attn_prompt(q3, k3, v3, prm["da_lambda"][j], prm["da_subln"][j], lam_init)
            else:
                cache_k, cache_v, page_table = page_info
                o = diff_attn_cached(q3, k3, v3, cache_k, cache_v, j, page_table, prm["da_lambda"][j],
                                     prm["da_subln"][j], lam_init, t_real)
            mix = matmul(o.reshape(m, d), w["da_w_o"], layer=j)
            out["k"].append(k3)
            out["v"].append(v3)
        h, xn = resid_norm(h, mix, prm["norm_mix_post"][i], prm["norm_mem_pre"][i])
        mk, mv = states["mem_k"][i], states["mem_v"][i]
        xw_ = mk.shape[-1] * mk.shape[-2]
        q = matmul(xn, w["xa_w_q"], bf16, layer=i)
        o = mem_attend(q.reshape(b, t, xw_), mk.reshape(b, -1, xw_), mv.reshape(b, -1, xw_))
        xa_out = matmul(o.reshape(m, xw_), w["xa_w_o"], layer=i)
        h, xn = resid_norm(h, xa_out, prm["norm_mem_post"][i], prm["norm_ffn_pre"][i])
        act, conv_state = ffn_up(xn, w["ffn_w_up"], i, prm["ffn_conv_w"][i], prm["ffn_conv_b"][i],
                                 states["conv"][i], t, t_real)
        f = matmul(act, w["ffn_w_down"], layer=i)
        out["conv"].append(conv_state)
        nxt = i + 1
        g_next = prm["norm_mix_pre"][nxt] if (nxt < depth and nxt % N_MIXERS != 0) else None
        h, xn = resid_norm(h, f, prm["norm_ffn_post"][i], g_next)
    return h.reshape(b, t, d), out


def kernel(x_prompt, x_sample, cache_k, cache_v, state_wkv, state_shift, cache_mem_k, cache_mem_v, state_ffn_conv, page_table, mem_prompt, norm_mix_pre, norm_mix_post, norm_mem_pre, norm_mem_post, norm_ffn_pre, norm_ffn_post, mem_norm, rw_mix, rw_w_r, rw_w_k, rw_w_v, rw_w_o, rw_w0, rw_w1, rw_w2, rw_a0, rw_a1, rw_a2, rw_v0, rw_v1, rw_v2, rw_g1, rw_g2, rw_k_k, rw_k_a, rw_r_k, rw_ln_w, rw_ln_b, da_w_qkv, da_w_o, da_lambda, da_subln, xa_w_q, xa_w_kv, xa_w_o, ffn_w_up, ffn_conv_w, ffn_conv_b, ffn_w_down):
    b, t, d = x_prompt.shape
    db, ds, _ = x_sample.shape
    depth = norm_mix_pre.shape[0]
    n_rwkv = state_wkv.shape[0]
    nh_rw = d // RW_HS
    f2 = ffn_w_up.shape[-1]
    past_len = page_table.shape[1] * cache_k.shape[2]

    prm = dict(norm_mix_pre=norm_mix_pre, norm_mix_post=norm_mix_post, norm_mem_pre=norm_mem_pre,
               norm_mem_post=norm_mem_post, norm_ffn_pre=norm_ffn_pre, norm_ffn_post=norm_ffn_post,
               rw_mix=rw_mix, rw_w0=rw_w0, rw_a0=rw_a0, rw_v0=rw_v0, rw_k_k=rw_k_k, rw_k_a=rw_k_a,
               rw_r_k=rw_r_k, rw_ln_w=rw_ln_w, rw_ln_b=rw_ln_b, da_lambda=da_lambda, da_subln=da_subln,
               ffn_conv_w=ffn_conv_w, ffn_conv_b=ffn_conv_b)

    ch_out = lambda wt: _to_ch(wt, nh_rw).astype(bf16)

    def lora(w_in, w_out, permute=True):
        rank = _round_up(w_in.shape[-1], LANES)
        w_out = _to_ch(w_out, nh_rw) if permute else w_out
        return _pad_to(w_in, 2, rank).astype(bf16), _pad_to(w_out, 1, rank).astype(bf16)

    w = dict(rw_w_r=ch_out(rw_w_r), rw_w_k=ch_out(rw_w_k), rw_w_v=ch_out(rw_w_v),
             rw_w_o=jnp.swapaxes(_to_ch(jnp.swapaxes(rw_w_o, 1, 2), nh_rw), 1, 2).astype(bf16),
             da_w_o=da_w_o.astype(bf16), xa_w_q=xa_w_q.astype(bf16),
             xa_w_kv=xa_w_kv.astype(bf16), xa_w_o=xa_w_o.astype(bf16), ffn_w_up=ffn_w_up,
             ffn_w_down=ffn_w_down.astype(bf16), da_w_qkv=da_w_qkv.astype(bf16))
    w["rw_w1"], w["rw_w2"] = lora(rw_w1, rw_w2)
    w["rw_a1"], w["rw_a2"] = lora(rw_a1, rw_a2)
    w["rw_v1"], w["rw_v2"] = lora(rw_v1, rw_v2)
    w["rw_g1"], w["rw_g2"] = lora(rw_g1, rw_g2)

    mem_rows = mem_prompt.reshape(-1, d)
    mem_k, mem_v = [], []
    for i in range(depth):
        kv = matmul(rms_norm_rows(mem_rows, mem_norm[i]), w["xa_w_kv"], layer=i)
        xw_ = kv.shape[1] // 2
        hd = xw_ // XA_HEADS
        mem_k.append(kv[:, :xw_].reshape(b, -1, XA_HEADS, hd))
        mem_v.append(kv[:, xw_:].reshape(b, -1, XA_HEADS, hd))

    states_p = dict(pos=jnp.arange(t), shift=jnp.zeros((n_rwkv, b, d), f32),
                    wkv=jnp.zeros((n_rwkv, b, nh_rw, RW_HS, RW_HS), f32), mem_k=mem_k, mem_v=mem_v,
                    conv=jnp.zeros((depth, b, CONV_W - 1, f2), f32))
    hp, out_p = _group(x_prompt, t, states_p, prm, w, None)

    tpad = _round_up(ds, SUBLANES)
    states_s = dict(pos=past_len + jnp.arange(tpad), shift=state_shift, wkv=state_wkv,
                    mem_k=cache_mem_k, mem_v=cache_mem_v, conv=state_ffn_conv)
    hs, out_s = _group(_pad_to(x_sample, 1, tpad), ds, states_s, prm, w, (cache_k, cache_v, page_table))

    nh_da = d // (2 * DA_HD)
    heads = lambda z, bb, tt: z.reshape(bb, tt, nh_da, 2 * DA_HD)
    return (hp, hs[:, :ds],
            jnp.stack([heads(z, b, t) for z in out_p["k"]]), jnp.stack([heads(z, b, t) for z in out_p["v"]]),
            jnp.stack([heads(z, db, tpad)[:, :ds] for z in out_s["k"]]),
            jnp.stack([heads(z, db, tpad)[:, :ds] for z in out_s["v"]]),
            jnp.stack(out_p["wkv"]), jnp.stack(out_s["wkv"]),
            jnp.stack(out_p["shift"]), jnp.stack(out_s["shift"]),
            jnp.stack(mem_k), jnp.stack(mem_v),
            jnp.stack(out_p["conv"]), jnp.stack(out_s["conv"]))
```

```python
import functools
import math

import jax
import jax.numpy as jnp
from jax import lax
from jax.experimental import pallas as pl
from jax.experimental.pallas import tpu as pltpu

f32 = jnp.float32
bf16 = jnp.bfloat16

NORM_EPS = 1e-6
NEG_INF = -1e30
N_MIXERS = 2
RW_HS = 64
RW_GN_EPS = 64e-5
DA_HD = 128
ROT_DIM = DA_HD // 4
ROPE_THETA = 500000.0
XA_HEADS = 4
CONV_W = 3

LANES = 128
SUBLANES = 8
VMEM_LIMIT_BYTES = 56 * 1024 * 1024

TILES = dict(
    norm_bm=256,
    mm_bm=1024, mm_bn=512, mm_bk=5504,
    mm_bn_wide=1024,
    mm_wide_max_k=4096,
    mix_bt=256,
    wkv_tc=16,
    attn_tq=512, attn_tk=512,
    mem_tq=512,
    ffn_bm=1024, ffn_tn=256,
    ffn_k_chunks=8,
)


def _cp(*sem):
    return pltpu.CompilerParams(dimension_semantics=sem, vmem_limit_bytes=VMEM_LIMIT_BYTES)


def _rms(x, g):
    return x * lax.rsqrt(jnp.mean(x * x, axis=-1, keepdims=True) + NORM_EPS) * g


def _pad_to(x, axis, size):
    if x.shape[axis] == size:
        return x
    pad = [(0, 0)] * x.ndim
    pad[axis] = (0, size - x.shape[axis])
    return jnp.pad(x, pad)


def _round_up(n, m):
    return (n + m - 1) // m * m


def _norm_kernel(x_ref, g_ref, o_ref):
    o_ref[...] = _rms(x_ref[...], g_ref[...]).astype(o_ref.dtype)


def rms_norm_rows(x, g, out_dtype=bf16):
    m, d = x.shape
    bm = min(TILES["norm_bm"], m)
    return pl.pallas_call(
        _norm_kernel,
        grid=(m // bm,),
        in_specs=[pl.BlockSpec((bm, d), lambda i: (i, 0)), pl.BlockSpec((1, d), lambda i: (0, 0))],
        out_specs=pl.BlockSpec((bm, d), lambda i: (i, 0)),
        out_shape=jax.ShapeDtypeStruct((m, d), out_dtype),
        compiler_params=_cp("parallel"), name="rms_norm",
    )(x, g.reshape(1, d))


def _resid_norm_kernel(x_ref, y_ref, gp_ref, gn_ref, xo_ref, xn_ref):
    xnew = x_ref[...] + _rms(y_ref[...], gp_ref[...])
    xo_ref[...] = xnew
    xn_ref[...] = _rms(xnew, gn_ref[...]).astype(xn_ref.dtype)


def _resid_kernel(x_ref, y_ref, gp_ref, xo_ref):
    xo_ref[...] = x_ref[...] + _rms(y_ref[...], gp_ref[...])


def resid_norm(x, y, g_post, g_next=None):
    m, d = x.shape
    bm = min(TILES["norm_bm"], m)
    row = pl.BlockSpec((bm, d), lambda i: (i, 0))
    gain = pl.BlockSpec((1, d), lambda i: (0, 0))
    if g_next is None:
        return pl.pallas_call(
            _resid_kernel, grid=(m // bm,), in_specs=[row, row, gain], out_specs=row,
            out_shape=jax.ShapeDtypeStruct((m, d), f32), compiler_params=_cp("parallel"), name="resid",
        )(x, y, g_post.reshape(1, d)), None
    return pl.pallas_call(
        _resid_norm_kernel, grid=(m // bm,), in_specs=[row, row, gain, gain], out_specs=[row, row],
        out_shape=[jax.ShapeDtypeStruct((m, d), f32), jax.ShapeDtypeStruct((m, d), bf16)],
        compiler_params=_cp("parallel"), name="resid_norm",
    )(x, y, g_post.reshape(1, d), g_next.reshape(1, d))


def _epi_none(acc):
    return acc


def _epi_tanh(acc):
    return jnp.tanh(acc)


def _epi_sigmoid(acc):
    return jax.nn.sigmoid(acc)


def _epi_rope(acc, c_ref, s1_ref, s2_ref):
    c, s1, s2 = c_ref[...], s1_ref[...], s2_ref[...]
    half = ROT_DIM // 2
    outs = []
    for grp in range(acc.shape[1] // DA_HD):
        x = acc[:, grp * DA_HD:(grp + 1) * DA_HD]
        outs.append(x * c + pltpu.roll(x, DA_HD - half, 1) * s1 + pltpu.roll(x, half, 1) * s2)
    return jnp.concatenate(outs, axis=1) if len(outs) > 1 else outs[0]


def _mm_kernel(*refs, nk, epilogue, n_extra):
    a_ref, b_ref = refs[0], refs[1]
    extra = refs[2:2 + n_extra]
    o_ref = refs[2 + n_extra]
    part = jnp.dot(a_ref[...], b_ref[...], preferred_element_type=f32)
    if nk == 1:
        o_ref[...] = epilogue(part, *extra).astype(o_ref.dtype)
        return
    acc_ref = refs[3 + n_extra]
    k = pl.program_id(2)

    @pl.when(k == 0)
    def _():
        acc_ref[...] = part

    @pl.when(k > 0)
    def _():
        acc_ref[...] += part

    @pl.when(k == nk - 1)
    def _():
        o_ref[...] = epilogue(acc_ref[...], *extra).astype(o_ref.dtype)


def _largest_tile(n, cap, unit):
    if n <= cap:
        return n
    best = None
    for t in range(unit, cap + 1, unit):
        if n % t == 0:
            best = t
    assert best is not None, (n, cap, unit)
    return best


def matmul(a, b, out_dtype=f32, epilogue=_epi_none, row_tables=(), table_rows=None, layer=None, cols=None):
    m, k = a.shape
    k2 = b.shape[-2]
    col0, n = cols if cols is not None else (0, b.shape[-1])
    assert k == k2 and a.dtype == bf16 and b.dtype == bf16 and (b.ndim == 3) == (layer is not None)
    bm = _largest_tile(m, TILES["mm_bm"], SUBLANES)
    bk = _largest_tile(k, TILES["mm_bk"], LANES)
    bn_cap = TILES["mm_bn_wide"] if (k <= TILES["mm_wide_max_k"] and not row_tables) else TILES["mm_bn"]
    bn = _largest_tile(n, bn_cap, LANES)
    nk = k // bk
    assert col0 % bn == 0
    jb = col0 // bn
    if layer is None:
        b_spec = pl.BlockSpec((bk, bn), lambda i, j, kk: (kk, j + jb))
    else:
        b_spec = pl.BlockSpec((None, bk, bn), lambda i, j, kk: (layer, kk, j + jb))
    in_specs = [pl.BlockSpec((bm, bk), lambda i, j, kk: (i, kk)), b_spec]
    tables = []
    for t in row_tables:
        if table_rows < bm:
            assert bm % table_rows == 0
            t = jnp.tile(t, (bm // table_rows, 1))
        assert t.shape[0] % bm == 0
        nrep = t.shape[0] // bm
        tables.append(t)
        in_specs.append(pl.BlockSpec((bm, t.shape[1]), lambda i, j, kk, nrep=nrep: (i % nrep, 0)))
    scratch = [pltpu.VMEM((bm, bn), f32)] if nk > 1 else []
    return pl.pallas_call(
        functools.partial(_mm_kernel, nk=nk, epilogue=epilogue, n_extra=len(tables)),
        grid=(m // bm, n // bn, nk),
        in_specs=in_specs,
        out_specs=pl.BlockSpec((bm, bn), lambda i, j, kk: (i, j)),
        out_shape=jax.ShapeDtypeStruct((m, n), out_dtype),
        scratch_shapes=scratch,
        compiler_params=_cp("parallel", "parallel", "arbitrary"), name="matmul_" + epilogue.__name__[5:],
    )(a, b, *tables)


def _mix_kernel(h_ref, g_ref, mix_ref, prev_ref, *refs, bt, last_tile, last_row):
    outs, last_ref, carry_ref = refs[:6], refs[6], refs[7]
    tj = pl.program_id(1)
    xn = _rms(h_ref[0], g_ref[...])

    @pl.when(tj == 0)
    def _():
        carry_ref[SUBLANES - 1:SUBLANES, :] = prev_ref[0]

    prev_row = carry_ref[SUBLANES - 1:SUBLANES, :]
    row = lax.broadcasted_iota(jnp.int32, xn.shape, 0)
    shifted = jnp.where(row == 0, prev_row, pltpu.roll(xn, 1, 0))
    xx = shifted - xn
    for m_i in range(6):
        outs[m_i][0] = (xn + xx * mix_ref[m_i:m_i + 1, :]).astype(outs[m_i].dtype)
    carry_ref[...] = xn[bt - SUBLANES:bt, :]

    @pl.when(tj == last_tile)
    def _():
        last_ref[0] = xn[last_row:last_row + 1, :]


def rwkv_mix(h, g, mix, x_prev, t_real):
    b, t, d = h.shape
    bt = min(TILES["mix_bt"], t)
    seq = pl.BlockSpec((1, bt, d), lambda bi, tj: (bi, tj, 0))
    one = pl.BlockSpec((1, 1, d), lambda bi, tj: (bi, 0, 0))
    outs = pl.pallas_call(
        functools.partial(_mix_kernel, bt=bt, last_tile=(t_real - 1) // bt, last_row=(t_real - 1) % bt),
        grid=(b, t // bt),
        in_specs=[seq, pl.BlockSpec((1, d), lambda bi, tj: (0, 0)),
                  pl.BlockSpec((6, d), lambda bi, tj: (0, 0)), one],
        out_specs=[seq] * 6 + [one],
        out_shape=[jax.ShapeDtypeStruct((b, t, d), bf16)] * 6 + [jax.ShapeDtypeStruct((b, 1, d), f32)],
        scratch_shapes=[pltpu.VMEM((SUBLANES, d), f32)],
        compiler_params=_cp("parallel", "arbitrary"), name="rwkv_mix",
    )(h, g.reshape(1, d), mix, x_prev.reshape(b, 1, d))
    return outs[:6], outs[6].reshape(b, d)


def _wkv_kernel(*refs, has_vres, tc, t_real, n_chunks):
    it = iter(refs)
    r_ref, k_ref, v_ref, wl_ref, al_ref, g_ref = (next(it) for _ in range(6))
    vl_ref, vf_ref = (next(it), next(it)) if has_vres else (None, None)
    w0_ref, a0_ref = next(it), next(it)
    v0_ref = next(it) if has_vres else None
    kk_ref, ka_ref, rk_ref, lnw_ref, lnb_ref, s0_ref = (next(it) for _ in range(6))
    yg_ref, sout_ref = next(it), next(it)
    s_scr = next(it)
    r_t, w_t, k_t, v_t, a_t, b_t, y_t = (next(it) for _ in range(7))
    ops_ct = [next(it) for _ in range(6)]
    v_ct = ops_ct[3]

    nvb = RW_HS // SUBLANES
    n_cp = RW_HS // 2
    ci = pl.program_id(1)

    @pl.when(ci == 0)
    def _():
        s_scr[...] = s0_ref[0]

    y_t[...] = jnp.zeros(y_t.shape, f32)

    low = lax.broadcasted_iota(jnp.int32, (tc, LANES), 1) < RW_HS

    def gather(ref, cp):
        cols = slice(cp * LANES, (cp + 1) * LANES)
        g0, g1 = ref[0, :, cols], ref[1, :, cols]
        return (jnp.where(low, g0, pltpu.roll(g1, RW_HS, 1)), jnp.where(low, pltpu.roll(g0, RW_HS, 1), g1))

    def scatter(x_even, x_odd):
        return (jnp.where(low, x_even, pltpu.roll(x_odd, RW_HS, 1)), jnp.where(low, pltpu.roll(x_even, RW_HS, 1), x_odd))

    def prow(ref, c):
        return ref[c:c + 1, :]

    ss = jnp.zeros((tc, LANES), f32)
    for cp in range(n_cp):
        for par, kc in enumerate(gather(k_ref, cp)):
            kk = kc * prow(kk_ref, 2 * cp + par)
            ss = ss + kk * kk
    kk_den = jnp.maximum(jnp.sqrt(ss), 1e-12)

    coef = jnp.zeros((tc, LANES), f32)
    for cp in range(n_cp):
        rr, kr, vr, wlr, alr = (gather(ref, cp) for ref in (r_ref, k_ref, v_ref, wl_ref, al_ref))
        if has_vres:
            vlr, vfr = gather(vl_ref, cp), gather(vf_ref, cp)
        for par in range(2):
            c = 2 * cp + par
            r, k, v = rr[par], kr[par], vr[par]
            z = -(prow(w0_ref, c) + wlr[par])
            softplus = jnp.maximum(z, 0.0) + jnp.log1p(jnp.exp(-jnp.abs(z)))
            w = jnp.exp(-jnp.exp(-softplus - 0.5))
            a = jax.nn.sigmoid(prow(a0_ref, c) + alr[par])
            if has_vres:
                v = v + (vfr[par] - v) * jax.nn.sigmoid(prow(v0_ref, c) + vlr[par])
            kk = k * prow(kk_ref, c) / kk_den
            k2 = k * (1.0 + (a - 1.0) * prow(ka_ref, c))
            coef = coef + r * k2 * prow(rk_ref, c)
            for dst, val in zip(ops_ct, (r, w, k2, v, -kk, kk * a)):
                dst[c * tc:(c + 1) * tc, :] = val

    for src, dst in zip(ops_ct, (r_t, w_t, k_t, v_t, a_t, b_t)):
        for tt in range(tc):
            dst[tt * RW_HS:(tt + 1) * RW_HS, :] = src[pl.ds(tt, RW_HS, stride=tc), :]

    def bcast_row(tile, i):
        return jnp.broadcast_to(tile[i:i + 1, :], (SUBLANES, LANES))

    def step(t, carry):
        base = pl.multiple_of(t * RW_HS, RW_HS)
        acc = [[jnp.zeros((SUBLANES, LANES), f32) for _ in range(2)] for _ in range(nvb)]
        for kb in range(nvb):
            a_tile = a_t[pl.ds(base + kb * SUBLANES, SUBLANES), :]
            for i in range(SUBLANES):
                k_i = kb * SUBLANES + i
                a_k = bcast_row(a_tile, i)
                for vb in range(nvb):
                    acc[vb][i % 2] = acc[vb][i % 2] + s_scr[vb, k_i] * a_k
        sa = [acc[vb][0] + acc[vb][1] for vb in range(nvb)]
        vt = [v_t[pl.ds(base + vb * SUBLANES, SUBLANES), :] for vb in range(nvb)]
        yacc = [[jnp.zeros((SUBLANES, LANES), f32) for _ in range(2)] for _ in range(nvb)]
        for kb in range(nvb):
            rows = pl.ds(base + kb * SUBLANES, SUBLANES)
            w_tile, b_tile, k_tile, r_tile = w_t[rows, :], b_t[rows, :], k_t[rows, :], r_t[rows, :]
            for i in range(SUBLANES):
                k_i = kb * SUBLANES + i
                w_k, b_k, k_k, r_k = (bcast_row(tl, i) for tl in (w_tile, b_tile, k_tile, r_tile))
                for vb in range(nvb):
                    s = s_scr[vb, k_i] * w_k + sa[vb] * b_k + vt[vb] * k_k
                    s_scr[vb, k_i] = s
                    yacc[vb][i % 2] = yacc[vb][i % 2] + s * r_k
        for vb in range(nvb):
            y_t[pl.ds(base + vb * SUBLANES, SUBLANES), :] = yacc[vb][0] + yacc[vb][1]
        return carry

    if t_real % tc == 0:
        lax.fori_loop(0, tc, step, 0)
    else:
        lax.fori_loop(0, jnp.clip(t_real - ci * tc, 0, tc), step, 0)

    def y_tile(c):
        return y_t[pl.ds(c, tc, stride=RW_HS), :]

    mu = jnp.zeros((tc, LANES), f32)
    for c in range(RW_HS):
        mu = mu + y_tile(c)
    mu = mu * (1.0 / RW_HS)
    var = jnp.zeros((tc, LANES), f32)
    for c in range(RW_HS):
        d = y_tile(c) - mu
        var = var + d * d
    rstd = lax.rsqrt(var * (1.0 / RW_HS) + RW_GN_EPS)
    for cp in range(n_cp):
        gates = gather(g_ref, cp)
        outs = []
        for par in range(2):
            c = 2 * cp + par
            yn = (y_tile(c) - mu) * rstd * prow(lnw_ref, c) + prow(lnb_ref, c)
            outs.append((yn + coef * v_ct[c * tc:(c + 1) * tc, :]) * gates[par])
        o0, o1 = scatter(outs[0], outs[1])
        cols = slice(cp * LANES, (cp + 1) * LANES)
        yg_ref[0, :, cols] = o0.astype(yg_ref.dtype)
        yg_ref[1, :, cols] = o1.astype(yg_ref.dtype)

    @pl.when(ci == n_chunks - 1)
    def _():
        sout_ref[0] = s_scr[...]


def _to_ch(x, nh):
    lead = x.shape[:-1]
    return jnp.swapaxes(x.reshape(*lead, nh, RW_HS), -1, -2).reshape(*lead, nh * RW_HS)


def _state_to_tiles(s):
    b, nh = s.shape[:2]
    s = s.reshape(b // 2, 2, nh, RW_HS // SUBLANES, SUBLANES, RW_HS)
    return s.transpose(0, 3, 5, 4, 1, 2).reshape(b // 2, RW_HS // SUBLANES, RW_HS, SUBLANES, 2 * nh)


def _state_from_tiles(st, nh):
    nb = st.shape[0]
    s = st.reshape(nb, RW_HS // SUBLANES, RW_HS, SUBLANES, 2, nh).transpose(0, 4, 5, 1, 3, 2)
    return s.reshape(nb * 2, nh, RW_HS, RW_HS)


def wkv7(r, k, v, wl, al, g, vres, rows, s0, t_real):
    b, t, d = r.shape
    nh = d // RW_HS
    assert 2 * nh == LANES and b % 2 == 0
    tc = min(TILES["wkv_tc"], t)
    assert tc % SUBLANES == 0 and t % tc == 0
    n_chunks = t // tc
    assert t_real % tc == 0 or n_chunks == 1
    has_vres = vres is not None
    nvb = RW_HS // SUBLANES
    seq = pl.BlockSpec((2, tc, d), lambda bi, ci: (bi, ci, 0))
    prow = pl.BlockSpec((RW_HS, LANES), lambda bi, ci: (0, 0))
    st = pl.BlockSpec((1, nvb, RW_HS, SUBLANES, LANES), lambda bi, ci: (bi, 0, 0, 0, 0))
    seq_in = [r, k, v, wl, al, g] + (list(vres) if has_vres else [])
    names = ["w0", "a0"] + (["v0"] if has_vres else []) + ["k_k", "k_a", "r_k", "ln_w", "ln_b"]
    row_in = [jnp.tile(rows[nm].reshape(nh, RW_HS).T, (1, 2)) for nm in names]
    op_scr = pltpu.VMEM((tc * RW_HS, LANES), f32)
    yg, s_out = pl.pallas_call(
        functools.partial(_wkv_kernel, has_vres=has_vres, tc=tc, t_real=t_real, n_chunks=n_chunks),
        grid=(b // 2, n_chunks),
        in_specs=[seq] * len(seq_in) + [prow] * len(row_in) + [st],
        out_specs=[seq, st],
        out_shape=[jax.ShapeDtypeStruct((b, t, d), bf16),
                   jax.ShapeDtypeStruct((b // 2, nvb, RW_HS, SUBLANES, LANES), f32)],
        scratch_shapes=[pltpu.VMEM((nvb, RW_HS, SUBLANES, LANES), f32)] + [op_scr] * 13,
        compiler_params=_cp("parallel", "arbitrary"), name="wkv7",
    )(*seq_in, *row_in, _state_to_tiles(s0.astype(f32)))
    return yg, _state_from_tiles(s_out, nh)


def _diff_lambda(lam_ref, lam_init):
    lp = lam_ref[...]
    s1 = jnp.sum(lp[0:1] * lp[1:2], axis=-1, keepdims=True)
    s2 = jnp.sum(lp[2:3] * lp[3:4], axis=-1, keepdims=True)
    return jnp.exp(s1) - jnp.exp(s2) + lam_init


def _online_softmax_step(s, v_bf, m_ref, l_ref, acc_ref, c):
    m_prev = m_ref[c]
    m_new = jnp.maximum(m_prev, jnp.max(s, axis=-1, keepdims=True))
    alpha = jnp.exp(m_prev - m_new)
    p = jnp.exp(s - m_new)
    l_ref[c] = alpha * l_ref[c] + jnp.sum(p, axis=-1, keepdims=True)
    acc_ref[c] = alpha * acc_ref[c] + jnp.dot(p.astype(bf16), v_bf, preferred_element_type=f32)
    m_ref[c] = m_new


def _diff_prompt_kernel(q_ref, k_ref, v_ref, lam_ref, sub_ref, o_ref, m_ref, l_ref, acc_ref,
                        *, tq, tk, nk, lam_init, scale):
    i, j = pl.program_id(2), pl.program_id(3)
    q_first = i * tq
    k_first = j * tk

    @pl.when(j == 0)
    def _():
        m_ref[...] = jnp.full(m_ref.shape, NEG_INF, f32)
        l_ref[...] = jnp.zeros(l_ref.shape, f32)
        acc_ref[...] = jnp.zeros(acc_ref.shape, f32)

    def block(diagonal):
        v_bf = v_ref[0].astype(bf16)
        for c in range(2):
            qc = q_ref[0, :, c * DA_HD:(c + 1) * DA_HD]
            kc = k_ref[0, :, c * DA_HD:(c + 1) * DA_HD].astype(bf16)
            s = lax.dot_general(qc, kc, (((1,), (1,)), ((), ())), preferred_element_type=f32) * scale
            if diagonal:
                q_pos = q_first + lax.broadcasted_iota(jnp.int32, (tq, tk), 0)
                k_pos = k_first + lax.broadcasted_iota(jnp.int32, (tq, tk), 1)
                s = jnp.where(k_pos <= q_pos, s, NEG_INF)
            _online_softmax_step(s, v_bf, m_ref, l_ref, acc_ref, c)

    @pl.when(k_first + tk - 1 <= q_first)
    def _():
        block(False)

    @pl.when((k_first + tk - 1 > q_first) & (k_first <= q_first + tq - 1))
    def _():
        block(True)

    @pl.when(j == nk - 1)
    def _():
        lam = _diff_lambda(lam_ref, lam_init)
        o = acc_ref[0] / l_ref[0] - lam * (acc_ref[1] / l_ref[1])
        o_ref[0] = (_rms(o, sub_ref[...]) * (1.0 - lam_init)).astype(o_ref.dtype)


def diff_attn_prompt(q, k, v, lam_params, subln, lam_init):
    b, t, d = q.shape
    hw = 2 * DA_HD
    nh = d // hw
    tq = min(TILES["attn_tq"], t)
    tk = min(TILES["attn_tk"], t)
    nq, nk = t // tq, t // tk
    qspec = pl.BlockSpec((1, tq, hw), lambda bi, hi, i, j: (bi, i, hi))
    kspec = pl.BlockSpec((1, tk, hw), lambda bi, hi, i, j: (bi, jnp.minimum(j, (i * tq + tq - 1) // tk), hi))
    return pl.pallas_call(
        functools.partial(_diff_prompt_kernel, tq=tq, tk=tk, nk=nk, lam_init=lam_init, scale=DA_HD ** -0.5),
        grid=(b, nh, nq, nk),
        in_specs=[qspec, kspec, kspec,
                  pl.BlockSpec((4, DA_HD), lambda bi, hi, i, j: (0, 0)),
                  pl.BlockSpec((1, hw), lambda bi, hi, i, j: (0, 0))],
        out_specs=qspec,
        out_shape=jax.ShapeDtypeStruct((b, t, d), bf16),
        scratch_shapes=[pltpu.VMEM((2, tq, 1), f32), pltpu.VMEM((2, tq, 1), f32), pltpu.VMEM((2, tq, hw), f32)],
        compiler_params=_cp("parallel", "parallel", "parallel", "arbitrary"), name="diff_attn_prompt",
    )(q, k, v, lam_params, subln.reshape(1, hw))


def _diff_cached_kernel(pt_ref, q_ref, *refs, n_steps, pps, nh, tpad, t_real, lam_init, scale):
    kp_refs, vp_refs = refs[:pps], refs[pps:2 * pps]
    kn_ref, vn_ref, lam_ref, sub_ref, o_ref, m_ref, l_ref, acc_ref = refs[2 * pps:]
    p = pl.program_id(1)
    nq = nh * tpad

    @pl.when(p == 0)
    def _():
        m_ref[...] = jnp.full(m_ref.shape, NEG_INF, f32)
        l_ref[...] = jnp.zeros(l_ref.shape, f32)
        acc_ref[...] = jnp.zeros(acc_ref.shape, f32)

    def update(k3, v3, causal):
        nkeys = k3.shape[0]
        k2 = k3.reshape(nkeys * nh, 2 * DA_HD).astype(bf16)
        v2 = v3.reshape(nkeys * nh, 2 * DA_HD).astype(bf16)
        row = lax.broadcasted_iota(jnp.int32, (nq, nkeys * nh), 0)
        col = lax.broadcasted_iota(jnp.int32, (nq, nkeys * nh), 1)
        keep = (row // tpad) == (col % nh)
        if causal:
            key = col // nh
            keep = keep & (key <= row % tpad) & (key < t_real)
        ps = []
        for c in range(2):
            s = lax.dot_general(q_ref[0, c], k2[:, c * DA_HD:(c + 1) * DA_HD], (((1,), (1,)), ((), ())),
                                preferred_element_type=f32) * scale
            s = jnp.where(keep, s, NEG_INF)
            m_prev = m_ref[c]
            m_new = jnp.maximum(m_prev, jnp.max(s, axis=-1, keepdims=True))
            alpha = jnp.exp(m_prev - m_new)
            pr = jnp.exp(s - m_new)
            l_ref[c] = alpha * l_ref[c] + jnp.sum(pr, axis=-1, keepdims=True)
            acc_ref[c] = alpha * acc_ref[c]
            m_ref[c] = m_new
            ps.append(pr.astype(bf16))
        pv = jnp.dot(jnp.concatenate(ps, axis=0), v2, preferred_element_type=f32)
        acc_ref[0] += pv[:nq]
        acc_ref[1] += pv[nq:]

    @pl.when(p < n_steps)
    def _():
        for kp_ref, vp_ref in zip(kp_refs, vp_refs):
            update(kp_ref[...], vp_ref[...], False)

    @pl.when(p == n_steps)
    def _():
        update(kn_ref[0], vn_ref[0], True)
        lam = _diff_lambda(lam_ref, lam_init)
        o = acc_ref[0] / l_ref[0] - lam * (acc_ref[1] / l_ref[1])
        o_ref[0] = (_rms(o, sub_ref[...]) * (1.0 - lam_init)).astype(o_ref.dtype)


def diff_attn_cached(q, k_new, v_new, cache_k, cache_v, layer, page_table, lam_params, subln, lam_init, t_real):
    b, tpad, d = q.shape
    hw = 2 * DA_HD
    nh = d // hw
    n_pages = page_table.shape[1]
    page = cache_k.shape[2]
    nq = nh * tpad
    qm = q.reshape(b, tpad, nh, 2, DA_HD).transpose(0, 3, 2, 1, 4).reshape(b, 2, nq, DA_HD)
    kn = k_new.reshape(b, tpad, nh, hw)
    vn = v_new.reshape(b, tpad, nh, hw)
    pps = 2 if n_pages % 2 == 0 else 1
    n_steps = n_pages // pps
    last = n_steps - 1

    def pspec(which):
        return pl.BlockSpec((None, None, page, nh, hw),
                            lambda bi, p, pt: (layer, pt[bi, jnp.minimum(p, last) * pps + which], 0, 0, 0))

    pspecs = [pspec(which) for which in range(pps)]
    nspec = pl.BlockSpec((1, tpad, nh, hw), lambda bi, p, pt: (bi, 0, 0, 0))
    grid_spec = pltpu.PrefetchScalarGridSpec(
        num_scalar_prefetch=1,
        grid=(b, n_steps + 1),
        in_specs=[pl.BlockSpec((1, 2, nq, DA_HD), lambda bi, p, pt: (bi, 0, 0, 0))] + pspecs + pspecs
        + [nspec, nspec,
           pl.BlockSpec((4, DA_HD), lambda bi, p, pt: (0, 0)),
           pl.BlockSpec((1, hw), lambda bi, p, pt: (0, 0))],
        out_specs=pl.BlockSpec((1, nq, hw), lambda bi, p, pt: (bi, 0, 0)),
        scratch_shapes=[pltpu.VMEM((2, nq, 1), f32), pltpu.VMEM((2, nq, 1), f32), pltpu.VMEM((2, nq, hw), f32)],
    )
    o = pl.pallas_call(
        functools.partial(_diff_cached_kernel, n_steps=n_steps, pps=pps, nh=nh, tpad=tpad, t_real=t_real,
                          lam_init=lam_init, scale=DA_HD ** -0.5),
        grid_spec=grid_spec,
        out_shape=jax.ShapeDtypeStruct((b, nq, hw), bf16),
        compiler_params=_cp("parallel", "arbitrary"), name="diff_attn_cached",
    )(page_table, qm, *([cache_k] * pps), *([cache_v] * pps), kn, vn, lam_params, subln.reshape(1, hw))
    return o.reshape(b, nh, tpad, hw).transpose(0, 2, 1, 3).reshape(b, tpad, d)


def _mem_attn_kernel(q_ref, k_ref, v_ref, o_ref, *, hd, scale):
    for h in range(XA_HEADS):
        cols = slice(h * hd, (h + 1) * hd)
        s = lax.dot_general(q_ref[0, :, cols], k_ref[0, :, cols].astype(bf16), (((1,), (1,)), ((), ())),
                            preferred_element_type=f32) * scale
        e = jnp.exp(s - jnp.max(s, axis=-1, keepdims=True))
        p = e / jnp.sum(e, axis=-1, keepdims=True)
        o_ref[0, :, cols] = jnp.dot(p.astype(bf16), v_ref[0, :, cols].astype(bf16),
                                    preferred_element_type=f32).astype(o_ref.dtype)


def mem_attend(q, mk, mv):
    b, t, w = q.shape
    nm = mk.shape[1]
    tq = min(TILES["mem_tq"], t)
    hd = w // XA_HEADS
    qspec = pl.BlockSpec((1, tq, w), lambda bi, i: (bi, i, 0))
    mspec = pl.BlockSpec((1, nm, w), lambda bi, i: (bi, 0, 0))
    return pl.pallas_call(
        functools.partial(_mem_attn_kernel, hd=hd, scale=hd ** -0.5),
        grid=(b, t // tq),
        in_specs=[qspec, mspec, mspec],
        out_specs=qspec,
        out_shape=jax.ShapeDtypeStruct((b, t, w), bf16),
        compiler_params=_cp("parallel", "parallel"), name="mem_attend",
    )(q, mk, mv)


def _silu_gate(gate, val):
    return gate * jax.nn.sigmoid(gate) * val


def _ffn_up_short_kernel(a_ref, wg_ref, wv_ref, cwg_ref, cwv_ref, cbg_ref, cbv_ref, pg_ref, pv_ref,
                         act_ref, csg_ref, csv_ref, *, seq_len, n_sub, last_row):
    a = a_ref[...]
    row = lax.broadcasted_iota(jnp.int32, (seq_len, act_ref.shape[1]), 0)

    def half(w_ref, cw_ref, cb_ref, p_ref, cs_ref):
        h_all = jnp.dot(a, w_ref[...].astype(bf16), preferred_element_type=f32)
        cw = cw_ref[...]
        outs = []
        for s in range(n_sub):
            h = h_all[s * seq_len:(s + 1) * seq_len]
            p0, p1 = p_ref[s, 0:1, :], p_ref[s, 1:2, :]
            r1 = jnp.where(row == 0, p1, pltpu.roll(h, 1, 0))
            r2 = jnp.where(row == 0, p0, jnp.where(row == 1, p1, pltpu.roll(h, 2, 0)))
            outs.append(cb_ref[...] + r2 * cw[0:1] + r1 * cw[1:2] + h * cw[2:3])
            cs_ref[s] = h[last_row - 1:last_row + 1]
        return jnp.concatenate(outs, axis=0) if n_sub > 1 else outs[0]

    gate = half(wg_ref, cwg_ref, cbg_ref, pg_ref, csg_ref)
    val = half(wv_ref, cwv_ref, cbv_ref, pv_ref, csv_ref)
    act_ref[...] = _silu_gate(gate, val).astype(act_ref.dtype)


def _ffn_up_long_kernel(a_ref, wg_ref, wv_ref, cwg_ref, cwv_ref, cbg_ref, cbv_ref, pg_ref, pv_ref,
                        act_ref, csg_ref, csv_ref, hbuf0, hbuf1, carry_ref,
                        *, bm, tiles_per_seq, last_row, nn, k_chunks):
    halo = SUBLANES
    i, j = pl.program_id(0), pl.program_id(1)

    @pl.when((i == 0) & (j == 0))
    def _():
        hbuf1[...] = jnp.zeros(hbuf1.shape, f32)

    def work(fill, drain):
        jt = jnp.maximum(j - 1, 0)
        first = (i % tiles_per_seq) == 0
        halves = ((cwg_ref, cbg_ref, pg_ref, csg_ref), (cwv_ref, cbv_ref, pv_ref, csv_ref))
        for slot, (_, _, p_ref, _) in enumerate(halves):
            @pl.when(first)
            def _():
                drain[slot, halo - 2:halo, :] = p_ref[0]

            @pl.when(jnp.logical_not(first))
            def _():
                drain[slot, halo - 2:halo, :] = carry_ref[jt, slot]

        rc = min(bm, 64)

        def gate_rows(r0):
            cvals = []
            for slot, (cw_ref, cb_ref, _, _) in enumerate(halves):
                cw = cw_ref[...]
                cvals.append(cb_ref[...] + drain[slot, halo - 2 + r0:halo - 2 + r0 + rc, :] * cw[0:1]
                             + drain[slot, halo - 1 + r0:halo - 1 + r0 + rc, :] * cw[1:2]
                             + drain[slot, halo + r0:halo + r0 + rc, :] * cw[2:3])
            act_ref[r0:r0 + rc, :] = _silu_gate(cvals[0], cvals[1]).astype(act_ref.dtype)

        segs = [(slot, kq) for kq in range(k_chunks) for slot in range(2)]
        row_chunks = list(range(0, bm, rc))
        per_seg = -(-len(row_chunks) // len(segs))
        kc = a_ref.shape[1] // k_chunks
        for si, (slot, kq) in enumerate(segs):
            w_ref = (wg_ref, wv_ref)[slot]
            part = jnp.dot(a_ref[:, kq * kc:(kq + 1) * kc], w_ref[kq * kc:(kq + 1) * kc, :].astype(bf16),
                           preferred_element_type=f32)
            if kq == 0:
                fill[slot, halo:halo + bm, :] = part
            else:
                fill[slot, halo:halo + bm, :] += part
            for r0 in row_chunks[si * per_seg:(si + 1) * per_seg]:
                gate_rows(r0)
        for slot, (_, _, _, cs_ref) in enumerate(halves):
            cs_ref[0] = drain[slot, halo + last_row - 1:halo + last_row + 1, :]

        @pl.when(j > 0)
        def _():
            for slot in range(2):
                carry_ref[jt, slot] = drain[slot, halo + bm - 2:halo + bm, :]

    @pl.when(j % 2 == 0)
    def _():
        work(hbuf0, hbuf1)

    @pl.when(j % 2 == 1)
    def _():
        work(hbuf1, hbuf0)


def ffn_up(xn, w_up, layer, conv_w, conv_b, prev_rows, t_seq, t_real):
    m, d = xn.shape
    f2 = w_up.shape[2]
    dff = f2 // 2
    nseq = m // t_seq
    bm = _largest_tile(m, TILES["ffn_bm"], SUBLANES)
    tn = _largest_tile(dff, TILES["ffn_tn"], LANES)
    nn = dff // tn
    cw3 = conv_w.reshape(CONV_W, f2)
    cb2 = conv_b.reshape(1, f2)
    args = (xn, w_up, w_up, cw3, cw3, cb2, cb2, prev_rows, prev_rows)
    if bm >= t_seq:
        assert bm % t_seq == 0
        n_sub = bm // t_seq
        last_row = t_real - 1
        assert last_row >= CONV_W - 2
        tail_spec = pl.BlockSpec((n_sub, CONV_W - 1, tn), lambda i, j: (i, 0, j))
        tail_shape = jax.ShapeDtypeStruct((nseq, CONV_W - 1, dff), f32)
        pspec = lambda off: pl.BlockSpec((n_sub, CONV_W - 1, tn), lambda i, j: (i, 0, j + off))
        act, csg, csv = pl.pallas_call(
            functools.partial(_ffn_up_short_kernel, seq_len=t_seq, n_sub=n_sub, last_row=last_row),
            grid=(m // bm, nn),
            in_specs=[pl.BlockSpec((bm, d), lambda i, j: (i, 0)),
                      pl.BlockSpec((None, d, tn), lambda i, j: (layer, 0, j)),
                      pl.BlockSpec((None, d, tn), lambda i, j: (layer, 0, j + nn)),
                      pl.BlockSpec((CONV_W, tn), lambda i, j: (0, j)),
                      pl.BlockSpec((CONV_W, tn), lambda i, j: (0, j + nn)),
                      pl.BlockSpec((1, tn), lambda i, j: (0, j)),
                      pl.BlockSpec((1, tn), lambda i, j: (0, j + nn)),
                      pspec(0), pspec(nn)],
            out_specs=[pl.BlockSpec((bm, tn), lambda i, j: (i, j)), tail_spec, tail_spec],
            out_shape=[jax.ShapeDtypeStruct((m, dff), bf16), tail_shape, tail_shape],
            compiler_params=_cp("parallel", "parallel"), name="ffn_up_short",
        )(*args)
        return act, jnp.concatenate([csg, csv], axis=-1)

    assert t_seq % bm == 0
    tiles_per_seq = t_seq // bm
    last_tile, last_row = (t_real - 1) // bm, (t_real - 1) % bm
    assert last_tile == tiles_per_seq - 1 and last_row >= CONV_W - 2
    jm = lambda j: jnp.minimum(j, nn - 1)
    je = lambda j: jnp.maximum(j - 1, 0)
    seq_of = lambda i: i // tiles_per_seq
    tail_spec = pl.BlockSpec((1, CONV_W - 1, tn), lambda i, j: (i, 0, je(j)))
    tail_shape = jax.ShapeDtypeStruct((nseq * tiles_per_seq, CONV_W - 1, dff), f32)
    pspec = lambda off: pl.BlockSpec((1, CONV_W - 1, tn), lambda i, j: (seq_of(i), 0, je(j) + off))
    act, csg, csv = pl.pallas_call(
        functools.partial(_ffn_up_long_kernel, bm=bm, tiles_per_seq=tiles_per_seq, last_row=last_row, nn=nn,
                          k_chunks=TILES["ffn_k_chunks"] if d % (TILES["ffn_k_chunks"] * 2 * LANES) == 0 else 1),
        grid=(m // bm, nn + 1),
        in_specs=[pl.BlockSpec((bm, d), lambda i, j: (i, 0)),
                  pl.BlockSpec((None, d, tn), lambda i, j: (layer, 0, jm(j))),
                  pl.BlockSpec((None, d, tn), lambda i, j: (layer, 0, jm(j) + nn)),
                  pl.BlockSpec((CONV_W, tn), lambda i, j: (0, je(j))),
                  pl.BlockSpec((CONV_W, tn), lambda i, j: (0, je(j) + nn)),
                  pl.BlockSpec((1, tn), lambda i, j: (0, je(j))),
                  pl.BlockSpec((1, tn), lambda i, j: (0, je(j) + nn)),
                  pspec(0), pspec(nn)],
        out_specs=[pl.BlockSpec((bm, tn), lambda i, j: (i, je(j))), tail_spec, tail_spec],
        out_shape=[jax.ShapeDtypeStruct((m, dff), bf16), tail_shape, tail_shape],
        scratch_shapes=[pltpu.VMEM((2, SUBLANES + bm, tn), f32), pltpu.VMEM((2, SUBLANES + bm, tn), f32),
                        pltpu.VMEM((nn, 2, CONV_W - 1, tn), f32)],
        compiler_params=_cp("arbitrary", "arbitrary"), name="ffn_up_long",
    )(*args)
    tails = jnp.concatenate([csg, csv], axis=-1).reshape(nseq, tiles_per_seq, CONV_W - 1, f2)
    return act, tails[:, last_tile]


def _rope_tables(pos):
    half = ROT_DIM // 2
    inv_freq = ROPE_THETA ** (-jnp.arange(0, ROT_DIM, 2, dtype=f32) / ROT_DIM)
    ang = pos.astype(f32)[:, None] * inv_freq[None, :]
    cos, sin = jnp.cos(ang), jnp.sin(ang)
    n = pos.shape[0]
    z_half = jnp.zeros((n, half), f32)
    z_rest = jnp.zeros((n, DA_HD - ROT_DIM), f32)
    c = jnp.concatenate([cos, cos, jnp.ones((n, DA_HD - ROT_DIM), f32)], axis=1)
    s1 = jnp.concatenate([-sin, z_half, z_rest], axis=1)
    s2 = jnp.concatenate([z_half, sin, z_rest], axis=1)
    return c, s1, s2


def _group(x, t_real, states, prm, w, page_info):
    b, t, d = x.shape
    m = b * t
    depth = prm["norm_mix_pre"].shape[0]
    h = x.reshape(m, d)
    out = dict(k=[], v=[], wkv=[], shift=[], conv=[])
    v_first = None
    xn = None
    rope = _rope_tables(states["pos"])
    for i in range(depth):
        j = i // N_MIXERS
        if i % N_MIXERS == 0:
            mixes, last = rwkv_mix(h.reshape(b, t, d), prm["norm_mix_pre"][i], prm["rw_mix"][j],
                                   states["shift"][j], t_real)
            xr, xw, xk, xv, xa, xg = (z.reshape(m, d) for z in mixes)
            r = matmul(xr, w["rw_w_r"], layer=j)
            k = matmul(xk, w["rw_w_k"], layer=j)
            v = matmul(xv, w["rw_w_v"], layer=j)
            wl = matmul(matmul(xw, w["rw_w1"], bf16, _epi_tanh, layer=j), w["rw_w2"], layer=j)
            al = matmul(matmul(xa, w["rw_a1"], bf16, layer=j), w["rw_a2"], layer=j)
            g = matmul(matmul(xg, w["rw_g1"], bf16, _epi_sigmoid, layer=j), w["rw_g2"], layer=j)
            rows = {nm: prm["rw_" + nm][j] for nm in ("w0", "a0", "k_k", "k_a", "ln_w", "ln_b")}
            rows["r_k"] = prm["rw_r_k"][j].reshape(d)
            if j == 0:
                vres = None
                v_first = v
            else:
                vl = matmul(matmul(xv, w["rw_v1"], bf16, layer=j - 1), w["rw_v2"], layer=j - 1)
                vres = (vl.reshape(b, t, d), v_first.reshape(b, t, d))
                rows["v0"] = prm["rw_v0"][j - 1]
            to3 = lambda z: z.reshape(b, t, d)
            yg, s_fin = wkv7(to3(r), to3(k), to3(v), to3(wl), to3(al), to3(g), vres, rows,
                             states["wkv"][j], t_real)
            mix = matmul(yg.reshape(m, d), w["rw_w_o"], layer=j)
            out["wkv"].append(s_fin.astype(states["wkv"][j].dtype))
            out["shift"].append(last)
        else:
            lam_init = 0.8 - 0.6 * math.exp(-0.3 * i)
            wqkv = w["da_w_qkv"]
            q = matmul(xn, wqkv, bf16, _epi_rope, rope, t, layer=j, cols=(0, d))
            k = matmul(xn, wqkv, f32, _epi_rope, rope, t, layer=j, cols=(d, d))
            v = matmul(xn, wqkv, layer=j, cols=(2 * d, d))
            q3, k3, v3 = q.reshape(b, t, d), k.reshape(b, t, d), v.reshape(b, t, d)
            if page_info is None:
                o = diff_attn_prompt(q3, k3, v3, prm["da_lambda"][j], prm["da_subln"][j], lam_init)
            else:
                cache_k, cache_v, page_table = page_info
                o = diff_attn_cached(q3, k3, v3, cache_k, cache_v, j, page_table, prm["da_lambda"][j],
                                     prm["da_subln"][j], lam_init, t_real)
            mix = matmul(o.reshape(m, d), w["da_w_o"], layer=j)
            out["k"].append(k3)
            out["v"].append(v3)
        h, xn = resid_norm(h, mix, prm["norm_mix_post"][i], prm["norm_mem_pre"][i])
        mk, mv = states["mem_k"][i], states["mem_v"][i]
        xw_ = mk.shape[-1] * mk.shape[-2]
        q = matmul(xn, w["xa_w_q"], bf16, layer=i)
        o = mem_attend(q.reshape(b, t, xw_), mk.reshape(b, -1, xw_), mv.reshape(b, -1, xw_))
        xa_out = matmul(o.reshape(m, xw_), w["xa_w_o"], layer=i)
        h, xn = resid_norm(h, xa_out, prm["norm_mem_post"][i], prm["norm_ffn_pre"][i])
        act, conv_state = ffn_up(xn, w["ffn_w_up"], i, prm["ffn_conv_w"][i], prm["ffn_conv_b"][i],
                                 states["conv"][i], t, t_real)
        f = matmul(act, w["ffn_w_down"], layer=i)
        out["conv"].append(conv_state)
        nxt = i + 1
        g_next = prm["norm_mix_pre"][nxt] if (nxt < depth and nxt % N_MIXERS != 0) else None
        h, xn = resid_norm(h, f, prm["norm_ffn_post"][i], g_next)
    return h.reshape(b, t, d), out


def kernel(x_prompt, x_sample, cache_k, cache_v, state_wkv, state_shift, cache_mem_k, cache_mem_v, state_ffn_conv, page_table, mem_prompt, norm_mix_pre, norm_mix_post, norm_mem_pre, norm_mem_post, norm_ffn_pre, norm_ffn_post, mem_norm, rw_mix, rw_w_r, rw_w_k, rw_w_v, rw_w_o, rw_w0, rw_w1, rw_w2, rw_a0, rw_a1, rw_a2, rw_v0, rw_v1, rw_v2, rw_g1, rw_g2, rw_k_k, rw_k_a, rw_r_k, rw_ln_w, rw_ln_b, da_w_qkv, da_w_o, da_lambda, da_subln, xa_w_q, xa_w_kv, xa_w_o, ffn_w_up, ffn_conv_w, ffn_conv_b, ffn_w_down):
    b, t, d = x_prompt.shape
    db, ds, _ = x_sample.shape
    depth = norm_mix_pre.shape[0]
    n_rwkv = state_wkv.shape[0]
    nh_rw = d // RW_HS
    f2 = ffn_w_up.shape[-1]
    past_len = page_table.shape[1] * cache_k.shape[2]

    prm = dict(norm_mix_pre=norm_mix_pre, norm_mix_post=norm_mix_post, norm_mem_pre=norm_mem_pre,
               norm_mem_post=norm_mem_post, norm_ffn_pre=norm_ffn_pre, norm_ffn_post=norm_ffn_post,
               rw_mix=rw_mix, rw_w0=rw_w0, rw_a0=rw_a0, rw_v0=rw_v0, rw_k_k=rw_k_k, rw_k_a=rw_k_a,
               rw_r_k=rw_r_k, rw_ln_w=rw_ln_w, rw_ln_b=rw_ln_b, da_lambda=da_lambda, da_subln=da_subln,
               ffn_conv_w=ffn_conv_w, ffn_conv_b=ffn_conv_b)

    ch_out = lambda wt: _to_ch(wt, nh_rw).astype(bf16)

    def lora(w_in, w_out, permute=True):
        rank = _round_up(w_in.shape[-1], LANES)
        w_out = _to_ch(w_out, nh_rw) if permute else w_out
        return _pad_to(w_in, 2, rank).astype(bf16), _pad_to(w_out, 1, rank).astype(bf16)

    w = dict(rw_w_r=ch_out(rw_w_r), rw_w_k=ch_out(rw_w_k), rw_w_v=ch_out(rw_w_v),
             rw_w_o=jnp.swapaxes(_to_ch(jnp.swapaxes(rw_w_o, 1, 2), nh_rw), 1, 2).astype(bf16),
             da_w_o=da_w_o.astype(bf16), xa_w_q=xa_w_q.astype(bf16),
             xa_w_kv=xa_w_kv.astype(bf16), xa_w_o=xa_w_o.astype(bf16), ffn_w_up=ffn_w_up,
             ffn_w_down=ffn_w_down.astype(bf16), da_w_qkv=da_w_qkv.astype(bf16))
    w["rw_w1"], w["rw_w2"] = lora(rw_w1, rw_w2)
    w["rw_a1"], w["rw_a2"] = lora(rw_a1, rw_a2)
    w["rw_v1"], w["rw_v2"] = lora(rw_v1, rw_v2)
    w["rw_g1"], w["rw_g2"] = lora(rw_g1, rw_g2)

    mem_rows = mem_prompt.reshape(-1, d)
    mem_k, mem_v = [], []
    for i in range(depth):
        kv = matmul(rms_norm_rows(mem_rows, mem_norm[i]), w["xa_w_kv"], layer=i)
        xw_ = kv.shape[1] // 2
        hd = xw_ // XA_HEADS
        mem_k.append(kv[:, :xw_].reshape(b, -1, XA_HEADS, hd))
        mem_v.append(kv[:, xw_:].reshape(b, -1, XA_HEADS, hd))

    states_p = dict(pos=jnp.arange(t), shift=jnp.zeros((n_rwkv, b, d), f32),
                    wkv=jnp.zeros((n_rwkv, b, nh_rw, RW_HS, RW_HS), f32), mem_k=mem_k, mem_v=mem_v,
                    conv=jnp.zeros((depth, b, CONV_W - 1, f2), f32))
    hp, out_p = _group(x_prompt, t, states_p, prm, w, None)

    tpad = _round_up(ds, SUBLANES)
    states_s = dict(pos=past_len + jnp.arange(tpad), shift=state_shift, wkv=state_wkv,
                    mem_k=cache_mem_k, mem_v=cache_mem_v, conv=state_ffn_conv)
    hs, out_s = _group(_pad_to(x_sample, 1, tpad), ds, states_s, prm, w, (cache_k, cache_v, page_table))

    nh_da = d // (2 * DA_HD)
    heads = lambda z, bb, tt: z.reshape(bb, tt, nh_da, 2 * DA_HD)
    return (hp, hs[:, :ds],
            jnp.stack([heads(z, b, t) for z in out_p["k"]]), jnp.stack([heads(z, b, t) for z in out_p["v"]]),
            jnp.stack([heads(z, db, tpad)[:, :ds] for z in out_s["k"]]),
            jnp.stack([heads(z, db, tpad)[:, :ds] for z in out_s["v"]]),
            jnp.stack(out_p["wkv"]), jnp.stack(out_s["wkv"]),
            jnp.stack(out_p["shift"]), jnp.stack(out_s["shift"]),
            jnp.stack(mem_k), jnp.stack(mem_v),
            jnp.stack(out_p["conv"]), jnp.stack(out_s["conv"]))
```
